```python
import jax, jax.numpy as jnp
from jax import lax
import numpy as np

D_MODEL = 1024
BATCH = 4
SEQ = 8192
DEPTH = 1
DEC_BATCH = 2
DEC_SEQ = 16384
PAST_LEN = 128

SGU_CHUNK = 128
SGU_GROUPS = 8
SGU_WIDTH = 1024
SGU_GROUP_DIM = SGU_WIDTH // SGU_GROUPS
MLA_HEADS = 8
QK_NOPE = 128
QK_ROPE = 64
V_HEAD = 128
Q_LORA = 384
KV_LORA = 256
ROPE_THETA = 10000.0
Q_BLOCK = 128
PEER_HEADS = 8
N_KEYS = 128
N_EXPERTS = N_KEYS * N_KEYS
PEER_QDIM = 256
PEER_HALF = PEER_QDIM // 2
PEER_TOPK = 16
TOKEN_BLOCK = 128
EPS = 1e-6

SPLIT_POINTS = (Q_LORA,
                Q_LORA + KV_LORA,
                Q_LORA + KV_LORA + QK_ROPE,
                Q_LORA + KV_LORA + QK_ROPE + SGU_WIDTH,
                Q_LORA + KV_LORA + QK_ROPE + 2 * SGU_WIDTH,
                Q_LORA + KV_LORA + QK_ROPE + 2 * SGU_WIDTH + D_MODEL)
IN_WIDTH = Q_LORA + KV_LORA + QK_ROPE + 2 * SGU_WIDTH + 2 * D_MODEL

kernel_name = "hybrid_sgu_mla_peer_encoder"


def rmsnorm(x, g):
    x32 = x.astype(jnp.float32)
    y = x32 * lax.rsqrt(jnp.mean(x32 * x32, axis=-1, keepdims=True) + EPS)
    return y.astype(x.dtype) * g


def rope_tables(seq, dtype):
    inv = 1.0 / (ROPE_THETA ** (jnp.arange(0, QK_ROPE, 2, dtype=jnp.float32) / QK_ROPE))
    ang = jnp.arange(seq, dtype=jnp.float32)[:, None] * inv[None, :]
    return jnp.cos(ang).astype(dtype), jnp.sin(ang).astype(dtype)


def apply_rope(x, cos, sin):
    x1, x2 = x[..., :QK_ROPE // 2], x[..., QK_ROPE // 2:]
    return jnp.concatenate([x1 * cos - x2 * sin, x2 * cos + x1 * sin], axis=-1)


def sgu_branch(u, v, g_norm, w_s, b_s):
    B, S, _ = u.shape
    u = jax.nn.gelu(u)
    v = rmsnorm(jax.nn.gelu(v), g_norm)
    v = v.reshape(B, S // SGU_CHUNK, SGU_CHUNK, SGU_GROUPS, SGU_GROUP_DIM)
    mixed = jnp.einsum('gpq,bnqgc->bnpgc', w_s, v) + b_s.T[None, None, :, :, None]
    return u * mixed.reshape(B, S, SGU_WIDTH)


def mla_branch(q_lat, kv_lat, k_rope, q_norm, kv_norm, w_uq, w_ukv):
    B, S, _ = q_lat.shape
    c_q = rmsnorm(q_lat, q_norm)
    c_kv = rmsnorm(kv_lat, kv_norm)
    q = (c_q @ w_uq).reshape(B, S, MLA_HEADS, QK_NOPE + QK_ROPE)
    kv = (c_kv @ w_ukv).reshape(B, S, MLA_HEADS, QK_NOPE + V_HEAD)
    q_nope, q_pe = q[..., :QK_NOPE], q[..., QK_NOPE:]
    k_nope, v = kv[..., :QK_NOPE], kv[..., QK_NOPE:]
    cos, sin = rope_tables(S, q.dtype)
    q_pe = apply_rope(q_pe, cos[:, None, :], sin[:, None, :])
    k_pe = apply_rope(k_rope, cos, sin)
    scale = (QK_NOPE + QK_ROPE) ** -0.5
    nb = S // Q_BLOCK
    qn_blocks = q_nope.reshape(B, nb, Q_BLOCK, MLA_HEADS, QK_NOPE).transpose(1, 0, 2, 3, 4)
    qp_blocks = q_pe.reshape(B, nb, Q_BLOCK, MLA_HEADS, QK_ROPE).transpose(1, 0, 2, 3, 4)

    def attend(blk):
        qn, qp = blk
        s = (jnp.einsum('bqhd,bkhd->bhqk', qn, k_nope)
             + jnp.einsum('bqhr,bkr->bhqk', qp, k_pe))
        p = jax.nn.softmax(s.astype(jnp.float32) * scale, axis=-1).astype(v.dtype)
        return jnp.einsum('bhqk,bkhd->bqhd', p, v)

    o = lax.map(attend, (qn_blocks, qp_blocks))
    return o.transpose(1, 0, 2, 3, 4).reshape(B, S, MLA_HEADS * V_HEAD)


def peer(xn, w_q, sub_keys, u_tab, v_tab):
    B, S, D = xn.shape
    xt = xn.reshape((B * S) // TOKEN_BLOCK, TOKEN_BLOCK, D)

    def block(xb):
        q = (xb @ w_q).reshape(TOKEN_BLOCK, PEER_HEADS, 2, PEER_HALF)
        s = jnp.einsum('thpc,pkc->thpk', q, sub_keys)
        s1, i1 = lax.top_k(s[:, :, 0], PEER_TOPK)
        s2, i2 = lax.top_k(s[:, :, 1], PEER_TOPK)
        cand = (s1[..., :, None] + s2[..., None, :]).reshape(TOKEN_BLOCK, PEER_HEADS, PEER_TOPK * PEER_TOPK)
        cid = (i1[..., :, None] * N_KEYS + i2[..., None, :]).reshape(TOKEN_BLOCK, PEER_HEADS, PEER_TOPK * PEER_TOPK)
        top_s, pos = lax.top_k(cand, PEER_TOPK)
        eid = jnp.take_along_axis(cid, pos, axis=-1)
        g = jax.nn.softmax(top_s.astype(jnp.float32), axis=-1).astype(xb.dtype)
        h = jax.nn.gelu(jnp.einsum('thkd,td->thk', u_tab[eid], xb))
        return jnp.einsum('thk,thkd->td', g * h, v_tab[eid])

    return lax.map(block, xt).reshape(B, S, D)


def trunk(x, norm_mix, w_in, q_norm, kv_norm, w_uq, w_ukv, w_o_b, sgu_norm, sgu_w, sgu_b,
          w_o_a, w_out, norm_ffn, peer_wq, peer_keys, peer_u, peer_v, norm_final):
    for l in range(DEPTH):
        xn = rmsnorm(x, norm_mix[l])
        z = xn @ w_in[l]
        q_lat, kv_lat, k_rope, u_a, v_a, g_a, g_b = jnp.split(z, SPLIT_POINTS, axis=-1)
        y_a = sgu_branch(u_a, v_a, sgu_norm[l], sgu_w[l], sgu_b[l]) @ w_o_a[l]
        y_b = mla_branch(q_lat, kv_lat, k_rope, q_norm[l], kv_norm[l], w_uq[l], w_ukv[l]) @ w_o_b[l]
        merged = jax.nn.sigmoid(g_a) * y_a + jax.nn.sigmoid(g_b) * y_b
        x = x + merged @ w_out[l]
        x = x + peer(rmsnorm(x, norm_ffn[l]), peer_wq[l], peer_keys[l], peer_u[l], peer_v[l])
    return rmsnorm(x, norm_final)


def setup_inputs(seed: int = 0) -> dict:
    key = jax.random.key(seed)
    ks = jax.random.split(key, 24)
    f32 = jnp.float32

    def nrm(k, shape, scale):
        return jax.random.normal(k, shape, f32) * scale

    def gain(k, shape):
        return 1.0 + 0.01 * jax.random.normal(k, shape, f32)

    L = DEPTH
    return {
        "x_prompt": jax.random.normal(ks[0], (BATCH, SEQ, D_MODEL), f32),
        "x_sample": jax.random.normal(ks[1], (DEC_BATCH, DEC_SEQ, D_MODEL), f32),
        "norm_mix": gain(ks[2], (L, D_MODEL)),
        "w_in": nrm(ks[3], (L, D_MODEL, IN_WIDTH), D_MODEL ** -0.5),
        "q_norm": gain(ks[4], (L, Q_LORA)),
        "kv_norm": gain(ks[5], (L, KV_LORA)),
        "w_uq": nrm(ks[6], (L, Q_LORA, MLA_HEADS * (QK_NOPE + QK_ROPE)), Q_LORA ** -0.5),
        "w_ukv": nrm(ks[7], (L, KV_LORA, MLA_HEADS * (QK_NOPE + V_HEAD)), KV_LORA ** -0.5),
        "w_o_b": nrm(ks[8], (L, MLA_HEADS * V_HEAD, D_MODEL), (MLA_HEADS * V_HEAD) ** -0.5),
        "sgu_norm": gain(ks[9], (L, SGU_WIDTH)),
        "sgu_w": nrm(ks[10], (L, SGU_GROUPS, SGU_CHUNK, SGU_CHUNK), SGU_CHUNK ** -0.5),
        "sgu_b": gain(ks[11], (L, SGU_GROUPS, SGU_CHUNK)),
        "w_o_a": nrm(ks[12], (L, SGU_WIDTH, D_MODEL), SGU_WIDTH ** -0.5),
        "w_out": nrm(ks[13], (L, D_MODEL, D_MODEL), D_MODEL ** -0.5),
        "norm_ffn": gain(ks[14], (L, D_MODEL)),
        "peer_wq": nrm(ks[15], (L, D_MODEL, PEER_HEADS * PEER_QDIM), D_MODEL ** -0.5),
        "peer_keys": nrm(ks[16], (L, 2, N_KEYS, PEER_HALF), PEER_HALF ** -0.5),
        "peer_u": nrm(ks[17], (L, N_EXPERTS, D_MODEL), D_MODEL ** -0.5),
        "peer_v": nrm(ks[18], (L, N_EXPERTS, D_MODEL), PEER_HEADS ** -0.5),
        "norm_final": gain(ks[19], (D_MODEL,)),
    }


def reference(x_prompt, x_sample, norm_mix, w_in, q_norm, kv_norm, w_uq, w_ukv, w_o_b, sgu_norm,
              sgu_w, sgu_b, w_o_a, w_out, norm_ffn, peer_wq, peer_keys, peer_u, peer_v, norm_final):
    y_prompt = trunk(x_prompt, norm_mix, w_in, q_norm, kv_norm, w_uq, w_ukv, w_o_b, sgu_norm, sgu_w, sgu_b,
                     w_o_a, w_out, norm_ffn, peer_wq, peer_keys, peer_u, peer_v, norm_final)
    y_sample = trunk(x_sample, norm_mix, w_in, q_norm, kv_norm, w_uq, w_ukv, w_o_b, sgu_norm, sgu_w, sgu_b,
                     w_o_a, w_out, norm_ffn, peer_wq, peer_keys, peer_u, peer_v, norm_final)
    return (y_prompt, y_sample)
```

```python
import functools

import jax
import jax.numpy as jnp
from jax import lax
from jax.experimental import pallas as pl
from jax.experimental.pallas import tpu as pltpu

F32 = jnp.float32
BF16 = jnp.bfloat16

D_MODEL = 1024
SGU_CHUNK = 128
SGU_GROUPS = 8
SGU_WIDTH = 1024
MLA_HEADS = 8
QK_NOPE = 128
QK_ROPE = 64
V_HEAD = 128
Q_LORA = 384
KV_LORA = 256
ROPE_THETA = 10000.0
PEER_HEADS = 8
N_KEYS = 128
PEER_HALF = 128
PEER_TOPK = 16
EPS = 1e-6

LANE = 128
QK_PAD = 256
LOG2E = 1.4426950408889634
VMEM_LIMIT = 56 * 1024 * 1024

FRONT_TM = 256
ATTN_TQ = 512
ATTN_TK = 512
POST_TM = 512
PEER_TM = 512
PEER_TE = 1024


def _rms(x, eps=EPS):
    return x * lax.rsqrt(jnp.mean(x * x, axis=-1, keepdims=True) + eps)


def _dot(a, b):
    return jnp.dot(a, b, preferred_element_type=F32)


def _dot_nt(a, b):
    return lax.dot_general(a, b, (((1,), (1,)), ((), ())), preferred_element_type=F32)


def _const_spec(shape):
    nd = len(shape)
    return pl.BlockSpec(shape, lambda *_: (0,) * nd, pipeline_mode=pl.Buffered(1))


def _front_kernel(x_ref, cos_ref, sin_ref, nmix_ref, wq_ref, wkv_ref, wkr_ref, wu_ref, wv_ref,
                  wga_ref, wgb_ref, qn_ref, kvn_ref, wuq_ref, wuqr_ref, wukv_ref, sgn_ref,
                  sw_ref, sbias_ref, woa_ref,
                  q_out, k_out, v_out, ya_out, sgb_out, vn_s, sgu_s):
    tm = x_ref.shape[1]
    x = x_ref[0]
    xn = (_rms(x) * nmix_ref[...]).astype(BF16)
    cos = cos_ref[...]
    sin = sin_ref[...]

    c_q = (_rms(_dot(xn, wq_ref[...])) * qn_ref[...]).astype(BF16)
    qa = _dot(c_q, wuq_ref[...])
    qr = _dot(c_q, wuqr_ref[...])
    for h in range(MLA_HEADS):
        q_out[0, :, h * QK_PAD:h * QK_PAD + LANE] = qa[:, h * QK_PAD:h * QK_PAD + LANE].astype(BF16)
        pe = qa[:, h * QK_PAD + LANE:(h + 1) * QK_PAD] * cos + qr[:, h * LANE:(h + 1) * LANE] * sin
        q_out[0, :, h * QK_PAD + LANE:(h + 1) * QK_PAD] = pe.astype(BF16)

    c_kv = (_rms(_dot(xn, wkv_ref[...])) * kvn_ref[...]).astype(BF16)
    kv = _dot(c_kv, wukv_ref[...])
    kr = _dot(xn, wkr_ref[...])
    kpe = (kr[:, :LANE] * cos + kr[:, LANE:] * sin).astype(BF16)
    for h in range(MLA_HEADS):
        k_out[0, :, h * QK_PAD:h * QK_PAD + LANE] = kv[:, h * 2 * LANE:h * 2 * LANE + LANE].astype(BF16)
        k_out[0, :, h * QK_PAD + LANE:(h + 1) * QK_PAD] = kpe
        v_out[0, :, h * LANE:(h + 1) * LANE] = kv[:, h * 2 * LANE + LANE:(h + 1) * 2 * LANE].astype(BF16)

    v = jax.nn.gelu(_dot(xn, wv_ref[...]))
    vn_s[...] = (_rms(v) * sgn_ref[...]).astype(BF16)
    u = jax.nn.gelu(_dot(xn, wu_ref[...]))
    for c in range(tm // SGU_CHUNK):
        rows = slice(c * SGU_CHUNK, (c + 1) * SGU_CHUNK)
        for g in range(SGU_GROUPS):
            cols = slice(g * LANE, (g + 1) * LANE)
            mixed = _dot(sw_ref[g], vn_s[rows, cols]) + sbias_ref[:, cols]
            sgu_s[rows, cols] = (u[rows, cols] * mixed).astype(BF16)
    ya = _dot(sgu_s[...], woa_ref[...])
    ya_out[0] = (jax.nn.sigmoid(_dot(xn, wga_ref[...])) * ya).astype(BF16)
    sgb_out[0] = jax.nn.sigmoid(_dot(xn, wgb_ref[...])).astype(BF16)


def _front(x, cos, sin, w, tm):
    B, S, D = x.shape
    assert S % tm == 0 and tm % SGU_CHUNK == 0
    nt = S // tm
    tok = lambda width: pl.BlockSpec((1, tm, width), lambda b, i: (b, i, 0))
    pos = pl.BlockSpec((tm, LANE), lambda b, i: (i, 0))
    consts = [w["nmix"], w["wq"], w["wkv"], w["wkr"], w["wu"], w["wv"], w["wga"], w["wgb"],
              w["qn"], w["kvn"], w["wuq"], w["wuqr"], w["wukv"], w["sgn"], w["sw"], w["sbias"],
              w["woa"]]
    out_shape = (
        jax.ShapeDtypeStruct((B, S, MLA_HEADS * QK_PAD), BF16),
        jax.ShapeDtypeStruct((B, S, MLA_HEADS * QK_PAD), BF16),
        jax.ShapeDtypeStruct((B, S, MLA_HEADS * V_HEAD), BF16),
        jax.ShapeDtypeStruct((B, S, D), BF16),
        jax.ShapeDtypeStruct((B, S, D), BF16),
    )
    return pl.pallas_call(
        _front_kernel,
        grid=(B, nt),
        in_specs=[tok(D), pos, pos] + [_const_spec(c.shape) for c in consts],
        out_specs=(tok(MLA_HEADS * QK_PAD), tok(MLA_HEADS * QK_PAD), tok(MLA_HEADS * V_HEAD),
                   tok(D), tok(D)),
        out_shape=out_shape,
        scratch_shapes=[pltpu.VMEM((tm, SGU_WIDTH), BF16), pltpu.VMEM((tm, SGU_WIDTH), BF16)],
        compiler_params=pltpu.CompilerParams(
            dimension_semantics=("parallel", "parallel"), vmem_limit_bytes=VMEM_LIMIT),
        name="front",
    )(x, cos, sin, *consts)


def _attn_kernel(q_ref, k_ref, v_ref, o_ref, *, tk):
    tq = q_ref.shape[1]
    nk = k_ref.shape[1] // tk
    q = q_ref[0]

    def body(j, carry):
        m, l, acc = carry
        start = pl.multiple_of(j * tk, tk)
        k = k_ref[0, pl.ds(start, tk), :]
        v = v_ref[0, pl.ds(start, tk), :]
        s = _dot_nt(q, k)
        m_new = jnp.maximum(m, jnp.max(s, axis=-1, keepdims=True))
        alpha = jnp.exp2(m - m_new)
        p = jnp.exp2(s - m_new)
        l = alpha * l + jnp.sum(p, axis=-1, keepdims=True)
        acc = alpha * acc + _dot(p.astype(BF16), v)
        return m_new, l, acc

    m0 = jnp.full((tq, 1), -jnp.inf, F32)
    l0 = jnp.zeros((tq, 1), F32)
    acc0 = jnp.zeros((tq, V_HEAD), F32)
    _, l, acc = lax.fori_loop(0, nk, body, (m0, l0, acc0))
    o_ref[0] = (acc / l).astype(o_ref.dtype)


def _attention(q, k, v, tq, tk):
    B, S, _ = q.shape
    assert S % tq == 0 and S % tk == 0
    return pl.pallas_call(
        functools.partial(_attn_kernel, tk=tk),
        grid=(B, MLA_HEADS, S // tq),
        in_specs=[
            pl.BlockSpec((1, tq, QK_PAD), lambda b, h, i: (b, i, h)),
            pl.BlockSpec((1, S, QK_PAD), lambda b, h, i: (b, 0, h)),
            pl.BlockSpec((1, S, V_HEAD), lambda b, h, i: (b, 0, h)),
        ],
        out_specs=pl.BlockSpec((1, tq, V_HEAD), lambda b, h, i: (b, i, h)),
        out_shape=jax.ShapeDtypeStruct((B, S, MLA_HEADS * V_HEAD), BF16),
        compiler_params=pltpu.CompilerParams(
            dimension_semantics=("parallel", "parallel", "arbitrary"),
            vmem_limit_bytes=VMEM_LIMIT),
        name="attn",
    )(q, k, v)


def _post_kernel(x_ref, o_ref, ya_ref, sgb_ref, wob_ref, wout_ref, x1_out):
    yb = _dot(o_ref[...], wob_ref[...])
    merged = ya_ref[...].astype(F32) + sgb_ref[...].astype(F32) * yb
    x1_out[...] = x_ref[...] + _dot(merged.astype(BF16), wout_ref[...])


def _post(x2, o2, ya2, sgb2, wob, wout, tm):
    T, D = x2.shape
    assert T % tm == 0
    tok = pl.BlockSpec((tm, D), lambda i: (i, 0))
    return pl.pallas_call(
        _post_kernel,
        grid=(T // tm,),
        in_specs=[tok, tok, tok, tok, _const_spec(wob.shape), _const_spec(wout.shape)],
        out_specs=tok,
        out_shape=jax.ShapeDtypeStruct((T, D), F32),
        compiler_params=pltpu.CompilerParams(
            dimension_semantics=("parallel",), vmem_limit_bytes=VMEM_LIMIT),
        name="post",
    )(x2, o2, ya2, sgb2, wob, wout)


def _col_max(a):
    return jnp.max(a, axis=0, keepdims=True)


def _top_values(e, n):
    vals = []
    for _ in range(n):
        m = jnp.maximum(_col_max(e), 0.0)
        vals.append(m)
        e = jnp.where(e == m, -1.0, e)
    return jnp.concatenate(vals, axis=0)


def _peer_kernel(x1_ref, nffn_ref, wqt_ref, keys_ref, u_ref, vt_ref, nfin_ref, y_out,
                 xn_s, qt_s, e1_s, e2_s, th_s, ht_s, a_s, acc_s, *, te):
    tm = x1_ref.shape[0]
    ng = tm // LANE
    nb = te // LANE
    j = pl.program_id(1)

    @pl.when(j == 0)
    def _route():
        xn = (_rms(x1_ref[...]) * nffn_ref[...]).astype(BF16)
        xn_s[...] = xn
        qt_s[...] = _dot_nt(wqt_ref[...], xn).astype(BF16)
        acc_s[...] = jnp.zeros_like(acc_s)
        for h in range(PEER_HEADS):
            for p, e_s in ((0, e1_s), (1, e2_s)):
                r0 = (2 * h + p) * PEER_HALF
                s = _dot(keys_ref[p], qt_s[r0:r0 + PEER_HALF, :])
                e = jnp.exp(s - _col_max(s))
                for g in range(ng):
                    e_s[g, h] = e[:, g * LANE:(g + 1) * LANE]

        def per_head(i, _):
            g = i // PEER_HEADS
            h = i % PEER_HEADS
            e1 = e1_s[g, h]
            e2 = e2_s[g, h]
            v1 = _top_values(e1, PEER_TOPK)
            v2 = _top_values(e2, PEER_TOPK)
            cand = jnp.concatenate([v1 * v2[b:b + 1] for b in range(PEER_TOPK)], axis=0)
            work = cand
            for _ in range(PEER_TOPK):
                th = jnp.maximum(_col_max(work), 0.0)
                work = jnp.where(work == th, -1.0, work)
            sel = cand >= th
            inv_z = 1.0 / jnp.sum(jnp.where(sel, cand, 0.0), axis=0, keepdims=True)
            e2n = e2 * inv_z
            e2_s[g, h] = e2n
            v2n = v2 * inv_z
            candn = jnp.concatenate([v1 * v2n[b:b + 1] for b in range(PEER_TOPK)], axis=0)
            thn = jnp.min(jnp.where(sel, candn, jnp.inf), axis=0, keepdims=True)
            th_s[g, h] = jnp.broadcast_to(thn, (8, LANE))
            return 0

        lax.fori_loop(0, ng * PEER_HEADS, per_head, 0)

    ht_s[...] = _dot_nt(u_ref[...], xn_s[...])

    def per_i1(i, _):
        i1 = j * nb + i
        r0 = pl.multiple_of(i * LANE, LANE)
        for g in range(ng):
            cols = slice(g * LANE, (g + 1) * LANE)
            w = jnp.zeros((LANE, LANE), F32)
            for h in range(PEER_HEADS):
                z = e1_s[g, h, pl.ds(i1, 1), :] * e2_s[g, h]
                w = w + jnp.where(z >= th_s[g, h, 0:1, :], z, 0.0)
            a_s[pl.ds(r0, LANE), cols] = (w * jax.nn.gelu(ht_s[pl.ds(r0, LANE), cols])).astype(BF16)
        return 0

    lax.fori_loop(0, nb, per_i1, 0)
    acc_s[...] += _dot(vt_ref[...], a_s[...])

    @pl.when(j == pl.num_programs(1) - 1)
    def _finish():
        x2 = x1_ref[...] + acc_s[...].T
        y_out[...] = _rms(x2) * nfin_ref[...]


def _peer(x1, w, tm, te):
    T, D = x1.shape
    NE = w["u"].shape[0]
    assert T % tm == 0 and NE % te == 0 and tm % LANE == 0 and te % LANE == 0
    ng = tm // LANE
    consts = [w["nffn"], w["wqt"], w["keys"]]
    return pl.pallas_call(
        functools.partial(_peer_kernel, te=te),
        grid=(T // tm, NE // te),
        in_specs=[pl.BlockSpec((tm, D), lambda i, j: (i, 0))]
        + [_const_spec(c.shape) for c in consts]
        + [pl.BlockSpec((te, D), lambda i, j: (j, 0)),
           pl.BlockSpec((D, te), lambda i, j: (0, j)),
           _const_spec(w["nfin"].shape)],
        out_specs=pl.BlockSpec((tm, D), lambda i, j: (i, 0)),
        out_shape=jax.ShapeDtypeStruct((T, D), F32),
        scratch_shapes=[
            pltpu.VMEM((tm, D), BF16),
            pltpu.VMEM((PEER_HEADS * 2 * PEER_HALF, tm), BF16),
            pltpu.VMEM((ng, PEER_HEADS, N_KEYS, LANE), F32),
            pltpu.VMEM((ng, PEER_HEADS, N_KEYS, LANE), F32),
            pltpu.VMEM((ng, PEER_HEADS, 8, LANE), F32),
            pltpu.VMEM((te, tm), F32),
            pltpu.VMEM((te, tm), BF16),
            pltpu.VMEM((D, tm), F32),
        ],
        compiler_params=pltpu.CompilerParams(
            dimension_semantics=("parallel", "arbitrary"), vmem_limit_bytes=VMEM_LIMIT),
        name="peer",
    )(x1, *consts, w["u"], w["vt"], w["nfin"])


def _rope_tables(seq):
    inv = 1.0 / (ROPE_THETA ** (jnp.arange(0, QK_ROPE, 2, dtype=F32) / QK_ROPE))
    ang = jnp.arange(seq, dtype=F32)[:, None] * inv[None, :]
    cos, sin = jnp.cos(ang), jnp.sin(ang)
    pad = jnp.zeros((seq, LANE - QK_ROPE), F32)
    return (jnp.concatenate([cos, cos, pad], axis=1), jnp.concatenate([sin, sin, pad], axis=1))


def _rot_cols(wpe):
    half = QK_ROPE // 2
    return jnp.concatenate([-wpe[..., half:], wpe[..., :half]], axis=-1)


def _prep_weights(norm_mix, w_in, q_norm, kv_norm, w_uq, w_ukv, w_o_b, sgu_norm, sgu_w, sgu_b,
                  w_o_a, w_out, norm_ffn, peer_wq, peer_keys, peer_u, peer_v, norm_final):
    a, b, c = Q_LORA, Q_LORA + KV_LORA, Q_LORA + KV_LORA + QK_ROPE
    d, e, f = c + SGU_WIDTH, c + 2 * SGU_WIDTH, c + 2 * SGU_WIDTH + D_MODEL
    zpad = lambda rows, n: jnp.zeros((rows, n), F32)
    w_kr = w_in[:, b:c]
    wkr = jnp.concatenate([w_kr, zpad(D_MODEL, LANE - QK_ROPE),
                           _rot_cols(w_kr), zpad(D_MODEL, LANE - QK_ROPE)], axis=1)
    uq = w_uq.reshape(Q_LORA, MLA_HEADS, QK_NOPE + QK_ROPE)
    hpad = jnp.zeros((Q_LORA, MLA_HEADS, LANE - QK_ROPE), F32)
    wuq = jnp.concatenate([uq, hpad], axis=-1).reshape(Q_LORA, MLA_HEADS * QK_PAD)
    wuqr = jnp.concatenate([_rot_cols(uq[..., QK_NOPE:]), hpad], axis=-1).reshape(Q_LORA, MLA_HEADS * LANE)
    scale = (QK_NOPE + QK_ROPE) ** -0.5 * LOG2E
    sbias = jnp.repeat(sgu_b.T, SGU_WIDTH // SGU_GROUPS, axis=1)
    row = lambda v: v.reshape(1, -1).astype(F32)
    return {
        "nmix": row(norm_mix), "wq": w_in[:, :a].astype(BF16), "wkv": w_in[:, a:b].astype(BF16),
        "wkr": wkr.astype(BF16), "wu": w_in[:, c:d].astype(BF16), "wv": w_in[:, d:e].astype(BF16),
        "wga": w_in[:, e:f].astype(BF16), "wgb": w_in[:, f:].astype(BF16),
        "qn": row(q_norm) * scale, "kvn": row(kv_norm),
        "wuq": wuq.astype(BF16), "wuqr": wuqr.astype(BF16), "wukv": w_ukv.astype(BF16),
        "sgn": row(sgu_norm), "sw": sgu_w.astype(BF16), "sbias": sbias.astype(F32),
        "woa": w_o_a.astype(BF16), "wob": w_o_b.astype(BF16), "wout": w_out.astype(BF16),
        "nffn": row(norm_ffn), "wqt": peer_wq.T.astype(BF16), "keys": peer_keys.astype(BF16),
        "u": peer_u.astype(BF16), "vt": peer_v.T.astype(BF16), "nfin": row(norm_final),
    }


def _trunk(x, w, front_tm=FRONT_TM, tq=ATTN_TQ, tk=ATTN_TK, post_tm=POST_TM,
           peer_tm=PEER_TM, peer_te=PEER_TE):
    B, S, D = x.shape
    cos, sin = _rope_tables(S)
    q, k, v, ya, sgb = _front(x, cos, sin, w, front_tm)
    o = _attention(q, k, v, tq, tk)
    flat = lambda t: t.reshape(B * S, t.shape[-1])
    x1 = _post(flat(x), flat(o), flat(ya), flat(sgb), w["wob"], w["wout"], post_tm)
    y = _peer(x1, w, peer_tm, peer_te)
    return y.reshape(B, S, D)


def kernel(x_prompt, x_sample, norm_mix, w_in, q_norm, kv_norm, w_uq, w_ukv, w_o_b, sgu_norm,
           sgu_w, sgu_b, w_o_a, w_out, norm_ffn, peer_wq, peer_keys, peer_u, peer_v, norm_final):
    assert norm_mix.shape[0] == 1, "single layer"
    w = _prep_weights(norm_mix[0], w_in[0], q_norm[0], kv_norm[0], w_uq[0], w_ukv[0], w_o_b[0],
                      sgu_norm[0], sgu_w[0], sgu_b[0], w_o_a[0], w_out[0], norm_ffn[0],
                      peer_wq[0], peer_keys[0], peer_u[0], peer_v[0], norm_final)
    return (_trunk(x_prompt, w), _trunk(x_sample, w))
```

```python
import functools

import jax
import jax.numpy as jnp
from jax import lax
from jax.experimental import pallas as pl
from jax.experimental.pallas import tpu as pltpu

F32 = jnp.float32
BF16 = jnp.bfloat16

D_MODEL = 1024
SGU_CHUNK = 128
SGU_GROUPS = 8
SGU_WIDTH = 1024
MLA_HEADS = 8
QK_NOPE = 128
QK_ROPE = 64
V_HEAD = 128
Q_LORA = 384
KV_LORA = 256
ROPE_THETA = 10000.0
PEER_HEADS = 8
N_KEYS = 128
PEER_HALF = 128
PEER_TOPK = 16
EPS = 1e-6

LANE = 128
BF16_ROWS = 16
QK_PAD = 256
LOG2E = 1.4426950408889634
VMEM_LIMIT = 56 * 1024 * 1024

FRONT_TM = 256
ATTN_TQ = 512
ATTN_TK = 1024
ATTN_UNROLL = 4
POST_TM = 512
PEER_TM = 512
PEER_TE = 1024


def _rms(x, eps=EPS):
    return x * lax.rsqrt(jnp.mean(x * x, axis=-1, keepdims=True) + eps)


def _dot(a, b):
    return jnp.dot(a, b, preferred_element_type=F32)


def _dot_nt(a, b):
    return lax.dot_general(a, b, (((1,), (1,)), ((), ())), preferred_element_type=F32)


def _const_spec(shape):
    nd = len(shape)
    return pl.BlockSpec(shape, lambda *_: (0,) * nd, pipeline_mode=pl.Buffered(1))


def _front_kernel(x_ref, cos_ref, sin_ref, nmix_ref, wq_ref, wkv_ref, wkr_ref, wu_ref, wv_ref,
                  wga_ref, wgb_ref, qn_ref, kvn_ref, wuq_ref, wuqr_ref, wuk_ref, wuvt_ref,
                  sgn_ref, sw_ref, sbias_ref, woa_ref,
                  q_out, k_out, vt_out, ya_out, sgb_out, vn_s, sgu_s):
    tm = x_ref.shape[1]
    x = x_ref[0]
    xn = (_rms(x) * nmix_ref[...]).astype(BF16)
    cos = cos_ref[...]
    sin = sin_ref[...]

    c_q = (_rms(_dot(xn, wq_ref[...])) * qn_ref[...]).astype(BF16)
    qa = _dot(c_q, wuq_ref[...])
    qr = _dot(c_q, wuqr_ref[...])
    for h in range(MLA_HEADS):
        q_out[0, :, h * QK_PAD:h * QK_PAD + LANE] = qa[:, h * QK_PAD:h * QK_PAD + LANE].astype(BF16)
        pe = qa[:, h * QK_PAD + LANE:(h + 1) * QK_PAD] * cos + qr[:, h * LANE:(h + 1) * LANE] * sin
        q_out[0, :, h * QK_PAD + LANE:(h + 1) * QK_PAD] = pe.astype(BF16)

    c_kv = (_rms(_dot(xn, wkv_ref[...])) * kvn_ref[...]).astype(BF16)
    kn = _dot(c_kv, wuk_ref[...])
    kr = _dot(xn, wkr_ref[...])
    kpe = (kr[:, :LANE] * cos + kr[:, LANE:] * sin).astype(BF16)
    for h in range(MLA_HEADS):
        k_out[0, :, h * QK_PAD:h * QK_PAD + LANE] = kn[:, h * LANE:(h + 1) * LANE].astype(BF16)
        k_out[0, :, h * QK_PAD + LANE:(h + 1) * QK_PAD] = kpe
    vt_out[0] = _dot_nt(wuvt_ref[...], c_kv).astype(BF16)

    v = jax.nn.gelu(_dot(xn, wv_ref[...]))
    vn_s[...] = (_rms(v) * sgn_ref[...]).astype(BF16)
    u = jax.nn.gelu(_dot(xn, wu_ref[...]))
    for c in range(tm // SGU_CHUNK):
        rows = slice(c * SGU_CHUNK, (c + 1) * SGU_CHUNK)
        for g in range(SGU_GROUPS):
            cols = slice(g * LANE, (g + 1) * LANE)
            mixed = _dot(sw_ref[g], vn_s[rows, cols]) + sbias_ref[:, cols]
            sgu_s[rows, cols] = (u[rows, cols] * mixed).astype(BF16)
    ya = _dot(sgu_s[...], woa_ref[...])
    ya_out[0] = (jax.nn.sigmoid(_dot(xn, wga_ref[...])) * ya).astype(BF16)
    sgb_out[0] = jax.nn.sigmoid(_dot(xn, wgb_ref[...])).astype(BF16)


def _front(x, cos, sin, w, tm):
    B, S, D = x.shape
    assert S % tm == 0 and tm % SGU_CHUNK == 0
    nt = S // tm
    tok = lambda width: pl.BlockSpec((1, tm, width), lambda b, i: (b, i, 0))
    pos = pl.BlockSpec((tm, LANE), lambda b, i: (i, 0))
    consts = [w["nmix"], w["wq"], w["wkv"], w["wkr"], w["wu"], w["wv"], w["wga"], w["wgb"],
              w["qn"], w["kvn"], w["wuq"], w["wuqr"], w["wuk"], w["wuvt"], w["sgn"], w["sw"],
              w["sbias"], w["woa"]]
    out_shape = (
        jax.ShapeDtypeStruct((B, S, MLA_HEADS * QK_PAD), BF16),
        jax.ShapeDtypeStruct((B, S, MLA_HEADS * QK_PAD), BF16),
        jax.ShapeDtypeStruct((B, MLA_HEADS * V_HEAD, S), BF16),
        jax.ShapeDtypeStruct((B, S, D), BF16),
        jax.ShapeDtypeStruct((B, S, D), BF16),
    )
    return pl.pallas_call(
        _front_kernel,
        grid=(B, nt),
        in_specs=[tok(D), pos, pos] + [_const_spec(c.shape) for c in consts],
        out_specs=(tok(MLA_HEADS * QK_PAD), tok(MLA_HEADS * QK_PAD),
                   pl.BlockSpec((1, MLA_HEADS * V_HEAD, tm), lambda b, i: (b, 0, i)),
                   tok(D), tok(D)),
        out_shape=out_shape,
        scratch_shapes=[pltpu.VMEM((tm, SGU_WIDTH), BF16), pltpu.VMEM((tm, SGU_WIDTH), BF16)],
        compiler_params=pltpu.CompilerParams(
            dimension_semantics=("parallel", "parallel"), vmem_limit_bytes=VMEM_LIMIT),
        name="front",
    )(x, cos, sin, *consts)


def _attn_kernel(q_ref, k_ref, vt_ref, o_ref, sa_s, sb_s, *, tk):
    tq = q_ref.shape[1]
    nk = k_ref.shape[1] // tk
    q = q_ref[0]

    def scores(s_ref, c):
        start = pl.multiple_of(c * tk, tk)
        s = _dot_nt(k_ref[0, pl.ds(start, tk), :], q)
        s_ref[...] = s
        return jnp.max(s, axis=0, keepdims=True)

    def update(s_ref, mc, c, carry):
        m, l, acc = carry
        start = pl.multiple_of(c * tk, tk)
        m_new = jnp.maximum(m, mc)
        alpha = jnp.exp2(m - m_new)
        p = jnp.exp2(s_ref[...] - m_new)
        l = alpha * l + jnp.sum(p, axis=0, keepdims=True)
        acc = alpha * acc + _dot(vt_ref[0, :, pl.ds(start, tk)], p.astype(BF16))
        return m_new, l, acc

    def run(c0, n, mc, state, more):
        bufs = (sa_s, sb_s)
        for t in range(n):
            mc_next = scores(bufs[(t + 1) % 2], c0 + t + 1) if (t + 1 < n or more) else None
            state = update(bufs[t % 2], mc, c0 + t, state)
            mc = mc_next
        return mc, state

    def body(i, carry):
        return run(i * ATTN_UNROLL, ATTN_UNROLL, *carry, more=True)

    state = (jnp.full((1, tq), -jnp.inf, F32), jnp.zeros((1, tq), F32), jnp.zeros((V_HEAD, tq), F32))
    nloop = nk // ATTN_UNROLL - 1
    mc, state = lax.fori_loop(0, nloop, body, (scores(sa_s, 0), state))
    _, (_, l, acc) = run(nloop * ATTN_UNROLL, ATTN_UNROLL, mc, state, more=False)
    o_ref[0] = (acc / l).T.astype(o_ref.dtype)


def _attention(q, k, vt, tq, tk):
    B, S, _ = q.shape
    assert S % tq == 0 and S % (ATTN_UNROLL * tk) == 0 and ATTN_UNROLL % 2 == 0
    return pl.pallas_call(
        functools.partial(_attn_kernel, tk=tk),
        scratch_shapes=[pltpu.VMEM((tk, tq), F32)] * 2,
        grid=(B, MLA_HEADS, S // tq),
        in_specs=[
            pl.BlockSpec((1, tq, QK_PAD), lambda b, h, i: (b, i, h)),
            pl.BlockSpec((1, S, QK_PAD), lambda b, h, i: (b, 0, h)),
            pl.BlockSpec((1, V_HEAD, S), lambda b, h, i: (b, h, 0)),
        ],
        out_specs=pl.BlockSpec((1, tq, V_HEAD), lambda b, h, i: (b, i, h)),
        out_shape=jax.ShapeDtypeStruct((B, S, MLA_HEADS * V_HEAD), BF16),
        compiler_params=pltpu.CompilerParams(
            dimension_semantics=("parallel", "parallel", "arbitrary"),
            vmem_limit_bytes=VMEM_LIMIT),
        name="attn",
    )(q, k, vt)


def _post_kernel(x_ref, o_ref, ya_ref, sgb_ref, wob_ref, wout_ref, x1_out):
    yb = _dot(o_ref[...], wob_ref[...])
    merged = ya_ref[...].astype(F32) + sgb_ref[...].astype(F32) * yb
    x1_out[...] = x_ref[...] + _dot(merged.astype(BF16), wout_ref[...])


def _post(x2, o2, ya2, sgb2, wob, wout, tm):
    T, D = x2.shape
    assert T % tm == 0
    tok = pl.BlockSpec((tm, D), lambda i: (i, 0))
    return pl.pallas_call(
        _post_kernel,
        grid=(T // tm,),
        in_specs=[tok, tok, tok, tok, _const_spec(wob.shape), _const_spec(wout.shape)],
        out_specs=tok,
        out_shape=jax.ShapeDtypeStruct((T, D), F32),
        compiler_params=pltpu.CompilerParams(
            dimension_semantics=("parallel",), vmem_limit_bytes=VMEM_LIMIT),
        name="post",
    )(x2, o2, ya2, sgb2, wob, wout)


def _col_max(a):
    return jnp.max(a, axis=0, keepdims=True)


def _top_values(e, n):
    vals = []
    for _ in range(n):
        m = jnp.maximum(_col_max(e), 0.0)
        vals.append(m)
        e = jnp.where(e == m, -1.0, e)
    return jnp.concatenate(vals, axis=0)


def _peer_kernel(x1_ref, nffn_ref, wqt_ref, keys_ref, u_ref, vt_ref, nfin_ref, y_out,
                 xn_s, qt_s, e1_s, e2_s, th_s, acc_s, *slabs, te):
    tm = x1_ref.shape[0]
    ng = tm // LANE
    nb = te // LANE
    j = pl.program_id(1)

    @pl.when(j == 0)
    def _route():
        xn = (_rms(x1_ref[...]) * nffn_ref[...]).astype(BF16)
        xn_s[...] = xn
        qt_s[...] = _dot_nt(wqt_ref[...], xn).astype(BF16)
        acc_s[...] = jnp.zeros_like(acc_s)
        for h in range(PEER_HEADS):
            for p, e_s in ((0, e1_s), (1, e2_s)):
                r0 = (2 * h + p) * PEER_HALF
                s = _dot(keys_ref[p], qt_s[r0:r0 + PEER_HALF, :])
                e = jnp.exp(s - _col_max(s))
                for g in range(ng):
                    e_s[g, h] = e[:, g * LANE:(g + 1) * LANE]

        def route_one(g, h):
            e1 = e1_s[g, h]
            e2 = e2_s[g, h]
            v1 = _top_values(e1, PEER_TOPK)
            v2 = _top_values(e2, PEER_TOPK)
            half = PEER_TOPK // 2

            def products(w2):
                return jnp.concatenate(
                    [v1 * w2[0:1]] + [v1[:half] * w2[b:b + 1] for b in range(1, PEER_TOPK)], axis=0)

            cand = products(v2)
            work = cand
            for _ in range(PEER_TOPK):
                th = jnp.maximum(_col_max(work), 0.0)
                work = jnp.where(work == th, -1.0, work)
            sel = cand >= th
            inv_z = 1.0 / jnp.sum(jnp.where(sel, cand, 0.0), axis=0, keepdims=True)
            e2_s[g, h] = e2 * inv_z
            thn = jnp.min(jnp.where(sel, products(v2 * inv_z), jnp.inf), axis=0, keepdims=True)
            th_s[g, h] = jnp.broadcast_to(thn, (8, LANE))

        def per_head_pair(i, _):
            for k in range(2):
                item = 2 * i + k
                route_one(item // PEER_HEADS, item % PEER_HEADS)
            return 0

        lax.fori_loop(0, ng * PEER_HEADS // 2, per_head_pair, 0)

    def hidden(c):
        slabs[2 * c][...] = _dot_nt(u_ref[c * 2 * LANE:(c + 1) * 2 * LANE, :], xn_s[...])

    def accumulate(c):
        acc_s[...] += _dot(vt_ref[:, c * 2 * LANE:(c + 1) * 2 * LANE], slabs[2 * c + 1][...])

    hidden(0)
    for c in range(nb // 2):
        ht_s, a_s = slabs[2 * c], slabs[2 * c + 1]
        if c + 1 < nb // 2:
            hidden(c + 1)
        for i in range(2):
            r = slice(i * LANE, (i + 1) * LANE)
            i1 = j * nb + 2 * c + i
            for g in range(ng):
                cols = slice(g * LANE, (g + 1) * LANE)
                w = jnp.zeros((LANE, LANE), F32)
                for h in range(PEER_HEADS):
                    z = e1_s[g, h, pl.ds(i1, 1), :] * e2_s[g, h]
                    w = w + jnp.where(z >= th_s[g, h, 0:1, :], z, 0.0)
                a_s[r, cols] = (w * jax.nn.gelu(ht_s[r, cols])).astype(BF16)
        if c >= 1:
            accumulate(c - 1)
    accumulate(nb // 2 - 1)

    @pl.when(j == pl.num_programs(1) - 1)
    def _finish():
        x2 = x1_ref[...] + acc_s[...].T
        y_out[...] = _rms(x2) * nfin_ref[...]


def _peer(x1, w, tm, te):
    T, D = x1.shape
    NE = w["u"].shape[0]
    assert T % tm == 0 and NE % te == 0 and tm % LANE == 0 and te % LANE == 0
    ng = tm // LANE
    consts = [w["nffn"], w["wqt"], w["keys"]]
    return pl.pallas_call(
        functools.partial(_peer_kernel, te=te),
        grid=(T // tm, NE // te),
        in_specs=[pl.BlockSpec((tm, D), lambda i, j: (i, 0))]
        + [_const_spec(c.shape) for c in consts]
        + [pl.BlockSpec((te, D), lambda i, j: (j, 0)),
           pl.BlockSpec((D, te), lambda i, j: (0, j)),
           _const_spec(w["nfin"].shape)],
        out_specs=pl.BlockSpec((tm, D), lambda i, j: (i, 0)),
        out_shape=jax.ShapeDtypeStruct((T, D), F32),
        scratch_shapes=[
            pltpu.VMEM((tm, D), BF16),
            pltpu.VMEM((PEER_HEADS * 2 * PEER_HALF, tm), BF16),
            pltpu.VMEM((ng, PEER_HEADS, N_KEYS, LANE), F32),
            pltpu.VMEM((ng, PEER_HEADS, N_KEYS, LANE), F32),
            pltpu.VMEM((ng, PEER_HEADS, 8, LANE), F32),
            pltpu.VMEM((D, tm), F32),
        ] + [pltpu.VMEM((2 * LANE, tm), F32),
             pltpu.VMEM((2 * LANE, tm), BF16)] * (te // (2 * LANE)),
        compiler_params=pltpu.CompilerParams(
            dimension_semantics=("parallel", "arbitrary"), vmem_limit_bytes=VMEM_LIMIT),
        name="peer",
    )(x1, *consts, w["u"], w["vt"], w["nfin"])


def _rope_tables(seq):
    inv = 1.0 / (ROPE_THETA ** (jnp.arange(0, QK_ROPE, 2, dtype=F32) / QK_ROPE))
    ang = jnp.arange(seq, dtype=F32)[:, None] * inv[None, :]
    cos, sin = jnp.cos(ang), jnp.sin(ang)
    pad = jnp.zeros((seq, LANE - QK_ROPE), F32)
    return (jnp.concatenate([cos, cos, pad], axis=1), jnp.concatenate([sin, sin, pad], axis=1))


def _rot_cols(wpe):
    half = QK_ROPE // 2
    return jnp.concatenate([-wpe[..., half:], wpe[..., :half]], axis=-1)


def _prep_weights(norm_mix, w_in, q_norm, kv_norm, w_uq, w_ukv, w_o_b, sgu_norm, sgu_w, sgu_b,
                  w_o_a, w_out, norm_ffn, peer_wq, peer_keys, peer_u, peer_v, norm_final):
    a, b, c = Q_LORA, Q_LORA + KV_LORA, Q_LORA + KV_LORA + QK_ROPE
    d, e, f = c + SGU_WIDTH, c + 2 * SGU_WIDTH, c + 2 * SGU_WIDTH + D_MODEL
    zpad = lambda rows, n: jnp.zeros((rows, n), F32)
    w_kr = w_in[:, b:c]
    wkr = jnp.concatenate([w_kr, zpad(D_MODEL, LANE - QK_ROPE),
                           _rot_cols(w_kr), zpad(D_MODEL, LANE - QK_ROPE)], axis=1)
    uq = w_uq.reshape(Q_LORA, MLA_HEADS, QK_NOPE + QK_ROPE)
    hpad = jnp.zeros((Q_LORA, MLA_HEADS, LANE - QK_ROPE), F32)
    wuq = jnp.concatenate([uq, hpad], axis=-1).reshape(Q_LORA, MLA_HEADS * QK_PAD)
    wuqr = jnp.concatenate([_rot_cols(uq[..., QK_NOPE:]), hpad], axis=-1).reshape(Q_LORA, MLA_HEADS * LANE)
    ukv = w_ukv.reshape(KV_LORA, MLA_HEADS, QK_NOPE + V_HEAD)
    scale = (QK_NOPE + QK_ROPE) ** -0.5 * LOG2E
    sbias = jnp.repeat(sgu_b.T, SGU_WIDTH // SGU_GROUPS, axis=1)
    row = lambda v: v.reshape(1, -1).astype(F32)
    return {
        "nmix": row(norm_mix), "wq": w_in[:, :a].astype(BF16), "wkv": w_in[:, a:b].astype(BF16),
        "wkr": wkr.astype(BF16), "wu": w_in[:, c:d].astype(BF16), "wv": w_in[:, d:e].astype(BF16),
        "wga": w_in[:, e:f].astype(BF16), "wgb": w_in[:, f:].astype(BF16),
        "qn": row(q_norm) * scale, "kvn": row(kv_norm),
        "wuq": wuq.astype(BF16), "wuqr": wuqr.astype(BF16),
        "wuk": ukv[..., :QK_NOPE].reshape(KV_LORA, MLA_HEADS * QK_NOPE).astype(BF16),
        "wuvt": ukv[..., QK_NOPE:].reshape(KV_LORA, MLA_HEADS * V_HEAD).T.astype(BF16),
        "sgn": row(sgu_norm), "sw": sgu_w.astype(BF16), "sbias": sbias.astype(F32),
        "woa": w_o_a.astype(BF16), "wob": w_o_b.astype(BF16), "wout": w_out.astype(BF16),
        "nffn": row(norm_ffn), "wqt": peer_wq.T.astype(BF16), "keys": peer_keys.astype(BF16),
        "u": peer_u.astype(BF16), "vt": peer_v.T.astype(BF16), "nfin": row(norm_final),
    }


def _trunk(x, w, front_tm=FRONT_TM, tq=ATTN_TQ, tk=ATTN_TK, post_tm=POST_TM,
           peer_tm=PEER_TM, peer_te=PEER_TE):
    B, S, D = x.shape
    cos, sin = _rope_tables(S)
    q, k, vt, ya, sgb = _front(x, cos, sin, w, front_tm)
    o = _attention(q, k, vt, tq, tk)
    flat = lambda t: t.reshape(B * S, t.shape[-1])
    x1 = _post(flat(x), flat(o), flat(ya), flat(sgb), w["wob"], w["wout"], post_tm)
    y = _peer(x1, w, peer_tm, peer_te)
    return y.reshape(B, S, D)


def kernel(x_prompt, x_sample, norm_mix, w_in, q_norm, kv_norm, w_uq, w_ukv, w_o_b, sgu_norm,
           sgu_w, sgu_b, w_o_a, w_out, norm_ffn, peer_wq, peer_keys, peer_u, peer_v, norm_final):
    assert norm_mix.shape[0] == 1, "single layer"
    w = _prep_weights(norm_mix[0], w_in[0], q_norm[0], kv_norm[0], w_uq[0], w_ukv[0], w_o_b[0],
                      sgu_norm[0], sgu_w[0], sgu_b[0], w_o_a[0], w_out[0], norm_ffn[0],
                      peer_wq[0], peer_keys[0], peer_u[0], peer_v[0], norm_final)
    return (_trunk(x_prompt, w), _trunk(x_sample, w))
```

```python
import functools

import jax
import jax.numpy as jnp
from jax import lax
from jax.experimental import pallas as pl
from jax.experimental.pallas import tpu as pltpu

F32 = jnp.float32
BF16 = jnp.bfloat16

D_MODEL = 1024
SGU_CHUNK = 128
SGU_GROUPS = 8
SGU_WIDTH = 1024
MLA_HEADS = 8
QK_NOPE = 128
QK_ROPE = 64
V_HEAD = 128
Q_LORA = 384
KV_LORA = 256
ROPE_THETA = 10000.0
PEER_HEADS = 8
N_KEYS = 128
PEER_HALF = 128
PEER_TOPK = 16
EPS = 1e-6

LANE = 128
BF16_ROWS = 16
QK_PAD = 256
LOG2E = 1.4426950408889634
VMEM_LIMIT = 56 * 1024 * 1024

FRONT_TM = 256
ATTN_TQ = 512
ATTN_TK = 1024
ATTN_UNROLL = 4
POST_TM = 512
PEER_TM = 512
PEER_TE = 1024
ROUTE_UNROLL = 4


def _rms(x, eps=EPS):
    return x * lax.rsqrt(jnp.mean(x * x, axis=-1, keepdims=True) + eps)


_GELU_K0 = -2.0 * 0.7978845608028654 * LOG2E
_GELU_K1 = _GELU_K0 * 0.044715


def _gelu(x):
    return x / (1.0 + jnp.exp2(x * (_GELU_K0 + _GELU_K1 * (x * x))))


def _dot(a, b):
    return jnp.dot(a, b, preferred_element_type=F32)


def _dot_nt(a, b):
    return lax.dot_general(a, b, (((1,), (1,)), ((), ())), preferred_element_type=F32)


def _const_spec(shape):
    nd = len(shape)
    return pl.BlockSpec(shape, lambda *_: (0,) * nd, pipeline_mode=pl.Buffered(1))


def _front_kernel(x_ref, cos_ref, sin_ref, nmix_ref, wq_ref, wkv_ref, wkr_ref, wu_ref, wv_ref,
                  wga_ref, wgb_ref, qn_ref, kvn_ref, wuq_ref, wuqr_ref, wuk_ref, wuvt_ref,
                  sgn_ref, sw_ref, sbias_ref, woa_ref,
                  q_out, k_out, vt_out, ya_out, sgb_out, vn_s, sgu_s):
    tm = x_ref.shape[1]
    x = x_ref[0]
    xn = (_rms(x) * nmix_ref[...]).astype(BF16)
    cos = cos_ref[...]
    sin = sin_ref[...]

    c_q = (_rms(_dot(xn, wq_ref[...])) * qn_ref[...]).astype(BF16)
    qa = _dot(c_q, wuq_ref[...])
    qr = _dot(c_q, wuqr_ref[...])
    for h in range(MLA_HEADS):
        q_out[0, :, h * QK_PAD:h * QK_PAD + LANE] = qa[:, h * QK_PAD:h * QK_PAD + LANE].astype(BF16)
        pe = qa[:, h * QK_PAD + LANE:(h + 1) * QK_PAD] * cos + qr[:, h * LANE:(h + 1) * LANE] * sin
        q_out[0, :, h * QK_PAD + LANE:(h + 1) * QK_PAD] = pe.astype(BF16)

    c_kv = (_rms(_dot(xn, wkv_ref[...])) * kvn_ref[...]).astype(BF16)
    kn = _dot(c_kv, wuk_ref[...])
    kr = _dot(xn, wkr_ref[...])
    kpe = (kr[:, :LANE] * cos + kr[:, LANE:] * sin).astype(BF16)
    for h in range(MLA_HEADS):
        k_out[0, :, h * QK_PAD:h * QK_PAD + LANE] = kn[:, h * LANE:(h + 1) * LANE].astype(BF16)
        k_out[0, :, h * QK_PAD + LANE:(h + 1) * QK_PAD] = kpe
    vt_out[0] = _dot_nt(wuvt_ref[...], c_kv).astype(BF16)

    v = _gelu(_dot(xn, wv_ref[...]))
    vn_s[...] = (_rms(v) * sgn_ref[...]).astype(BF16)
    u = _gelu(_dot(xn, wu_ref[...]))
    for c in range(tm // SGU_CHUNK):
        rows = slice(c * SGU_CHUNK, (c + 1) * SGU_CHUNK)
        for g in range(SGU_GROUPS):
            cols = slice(g * LANE, (g + 1) * LANE)
            mixed = _dot(sw_ref[g], vn_s[rows, cols]) + sbias_ref[:, cols]
            sgu_s[rows, cols] = (u[rows, cols] * mixed).astype(BF16)
    ya = _dot(sgu_s[...], woa_ref[...])
    ya_out[0] = (jax.nn.sigmoid(_dot(xn, wga_ref[...])) * ya).astype(BF16)
    sgb_out[0] = jax.nn.sigmoid(_dot(xn, wgb_ref[...])).astype(BF16)


def _front(x, cos, sin, w, tm):
    B, S, D = x.shape
    assert S % tm == 0 and tm % SGU_CHUNK == 0
    nt = S // tm
    tok = lambda width: pl.BlockSpec((1, tm, width), lambda b, i: (b, i, 0))
    pos = pl.BlockSpec((tm, LANE), lambda b, i: (i, 0))
    consts = [w["nmix"], w["wq"], w["wkv"], w["wkr"], w["wu"], w["wv"], w["wga"], w["wgb"],
              w["qn"], w["kvn"], w["wuq"], w["wuqr"], w["wuk"], w["wuvt"], w["sgn"], w["sw"],
              w["sbias"], w["woa"]]
    out_shape = (
        jax.ShapeDtypeStruct((B, S, MLA_HEADS * QK_PAD), BF16),
        jax.ShapeDtypeStruct((B, S, MLA_HEADS * QK_PAD), BF16),
        jax.ShapeDtypeStruct((B, MLA_HEADS * V_HEAD, S), BF16),
        jax.ShapeDtypeStruct((B, S, D), BF16),
        jax.ShapeDtypeStruct((B, S, D), BF16),
    )
    return pl.pallas_call(
        _front_kernel,
        grid=(B, nt),
        in_specs=[tok(D), pos, pos] + [_const_spec(c.shape) for c in consts],
        out_specs=(tok(MLA_HEADS * QK_PAD), tok(MLA_HEADS * QK_PAD),
                   pl.BlockSpec((1, MLA_HEADS * V_HEAD, tm), lambda b, i: (b, 0, i)),
                   tok(D), tok(D)),
        out_shape=out_shape,
        scratch_shapes=[pltpu.VMEM((tm, SGU_WIDTH), BF16), pltpu.VMEM((tm, SGU_WIDTH), BF16)],
        compiler_params=pltpu.CompilerParams(
            dimension_semantics=("parallel", "parallel"), vmem_limit_bytes=VMEM_LIMIT),
        name="front",
    )(x, cos, sin, *consts)


def _attn_kernel(q_ref, k_ref, vt_ref, o_ref, sa_s, sb_s, *, tk):
    tq = q_ref.shape[1]
    nk = k_ref.shape[1] // tk
    q = q_ref[0]

    def scores(s_ref, c):
        start = pl.multiple_of(c * tk, tk)
        s = _dot_nt(k_ref[0, pl.ds(start, tk), :], q)
        s_ref[...] = s
        return jnp.max(s, axis=0, keepdims=True)

    def update(s_ref, mc, c, carry):
        m, l, acc = carry
        start = pl.multiple_of(c * tk, tk)
        m_new = jnp.maximum(m, mc)
        alpha = jnp.exp2(m - m_new)
        p = jnp.exp2(s_ref[...] - m_new)
        l = alpha * l + jnp.sum(p, axis=0, keepdims=True)
        acc = alpha * acc + _dot(vt_ref[0, :, pl.ds(start, tk)], p.astype(BF16))
        return m_new, l, acc

    def run(c0, n, mc, state, more):
        bufs = (sa_s, sb_s)
        for t in range(n):
            mc_next = scores(bufs[(t + 1) % 2], c0 + t + 1) if (t + 1 < n or more) else None
            state = update(bufs[t % 2], mc, c0 + t, state)
            mc = mc_next
        return mc, state

    def body(i, carry):
        return run(i * ATTN_UNROLL, ATTN_UNROLL, *carry, more=True)

    state = (jnp.full((1, tq), -jnp.inf, F32), jnp.zeros((1, tq), F32), jnp.zeros((V_HEAD, tq), F32))
    nloop = nk // ATTN_UNROLL - 1
    mc, state = lax.fori_loop(0, nloop, body, (scores(sa_s, 0), state))
    _, (_, l, acc) = run(nloop * ATTN_UNROLL, ATTN_UNROLL, mc, state, more=False)
    o_ref[0] = (acc / l).T.astype(o_ref.dtype)


def _attention(q, k, vt, tq, tk):
    B, S, _ = q.shape
    assert S % tq == 0 and S % (ATTN_UNROLL * tk) == 0 and ATTN_UNROLL % 2 == 0
    return pl.pallas_call(
        functools.partial(_attn_kernel, tk=tk),
        scratch_shapes=[pltpu.VMEM((tk, tq), F32)] * 2,
        grid=(B, MLA_HEADS, S // tq),
        in_specs=[
            pl.BlockSpec((1, tq, QK_PAD), lambda b, h, i: (b, i, h)),
            pl.BlockSpec((1, S, QK_PAD), lambda b, h, i: (b, 0, h)),
            pl.BlockSpec((1, V_HEAD, S), lambda b, h, i: (b, h, 0)),
        ],
        out_specs=pl.BlockSpec((1, tq, V_HEAD), lambda b, h, i: (b, i, h)),
        out_shape=jax.ShapeDtypeStruct((B, S, MLA_HEADS * V_HEAD), BF16),
        compiler_params=pltpu.CompilerParams(
            dimension_semantics=("parallel", "parallel", "arbitrary"),
            vmem_limit_bytes=VMEM_LIMIT),
        name="attn",
    )(q, k, vt)


def _post_kernel(x_ref, o_ref, ya_ref, sgb_ref, wob_ref, wout_ref, x1_out):
    yb = _dot(o_ref[...], wob_ref[...])
    merged = ya_ref[...].astype(F32) + sgb_ref[...].astype(F32) * yb
    x1_out[...] = x_ref[...] + _dot(merged.astype(BF16), wout_ref[...])


def _post(x2, o2, ya2, sgb2, wob, wout, tm):
    T, D = x2.shape
    assert T % tm == 0
    tok = pl.BlockSpec((tm, D), lambda i: (i, 0))
    return pl.pallas_call(
        _post_kernel,
        grid=(T // tm,),
        in_specs=[tok, tok, tok, tok, _const_spec(wob.shape), _const_spec(wout.shape)],
        out_specs=tok,
        out_shape=jax.ShapeDtypeStruct((T, D), F32),
        compiler_params=pltpu.CompilerParams(
            dimension_semantics=("parallel",), vmem_limit_bytes=VMEM_LIMIT),
        name="post",
    )(x2, o2, ya2, sgb2, wob, wout)


def _col_max(a):
    return jnp.max(a, axis=0, keepdims=True)


def _sorting_network(n):
    comps = []
    p = 1
    while p < n:
        k = p
        while k >= 1:
            for j in range(k % p, n - k, 2 * k):
                for i in range(min(k, n - j - k)):
                    if (i + j) // (2 * p) == (i + j + k) // (2 * p):
                        comps.append((i + j, i + j + k))
            k //= 2
        p *= 2
    return comps


def _merge_heads(lists, extra, n):
    lists = list(lists)
    vals = []
    for t in range(n):
        head = lists[0] if extra is None else jnp.maximum(lists[0], extra)
        m = jnp.maximum(_col_max(head), 0.0)
        vals.append(m)
        if t == n - 1:
            break
        hit = lists[0] == m
        live = min(len(lists) - 1, n - 1 - t)
        for d in range(live):
            lists[d] = jnp.where(hit, lists[d + 1], lists[d])
        if live == len(lists) - 1:
            lists[live] = jnp.where(hit, -1.0, lists[live])
        if extra is not None:
            extra = jnp.where(extra == m, -1.0, extra)
    return vals


def _top_values(e, n):
    groups = [e[8 * r:8 * r + 8] for r in range(e.shape[0] // 8)]
    for a, b in _sorting_network(len(groups)):
        groups[a], groups[b] = jnp.maximum(groups[a], groups[b]), jnp.minimum(groups[a], groups[b])
    return jnp.concatenate(_merge_heads(groups[:n], None, n), axis=0)


def _peer_kernel(x1_ref, nffn_ref, wqt_ref, keys_ref, u_ref, vt_ref, nfin_ref, y_out,
                 xn_s, qt_s, e1_s, e2_s, th_s, acc_s, *slabs, te):
    tm = x1_ref.shape[0]
    ng = tm // LANE
    nb = te // LANE
    j = pl.program_id(1)

    @pl.when(j == 0)
    def _route():
        xn = (_rms(x1_ref[...]) * nffn_ref[...]).astype(BF16)
        xn_s[...] = xn
        qt_s[...] = _dot_nt(wqt_ref[...], xn).astype(BF16)
        acc_s[...] = jnp.zeros_like(acc_s)
        for h in range(PEER_HEADS):
            for p, e_s in ((0, e1_s), (1, e2_s)):
                r0 = (2 * h + p) * PEER_HALF
                s = _dot(keys_ref[p], qt_s[r0:r0 + PEER_HALF, :])
                e = jnp.exp(s - _col_max(s))
                for g in range(ng):
                    e_s[g, h] = e[:, g * LANE:(g + 1) * LANE]

        def route_one(g, h):
            e1 = e1_s[g, h]
            e2 = e2_s[g, h]
            v1 = _top_values(e1, PEER_TOPK)
            v2 = _top_values(e2, PEER_TOPK)
            half = PEER_TOPK // 2

            def products(w2):
                return ([v1[:half] * w2[b:b + 1] for b in range(PEER_TOPK)], v1[half:] * w2[0:1])

            cand = products(v2)
            th = _merge_heads(*cand, PEER_TOPK)[-1]
            sel = ([c >= th for c in cand[0]], cand[1] >= th)

            def selected(prod, fill, combine):
                out = jnp.where(sel[1], prod[1], fill)
                for s, c in zip(sel[0], prod[0]):
                    out = combine(out, jnp.where(s, c, fill))
                return out

            inv_z = 1.0 / jnp.sum(selected(cand, 0.0, jnp.add), axis=0, keepdims=True)
            e2_s[g, h] = e2 * inv_z
            thn = jnp.min(selected(products(v2 * inv_z), jnp.inf, jnp.minimum), axis=0, keepdims=True)
            th_s[g, h] = jnp.broadcast_to(thn, (8, LANE))

        def route_group(i, _):
            for k in range(ROUTE_UNROLL):
                item = ROUTE_UNROLL * i + k
                route_one(item // PEER_HEADS, item % PEER_HEADS)
            return 0

        lax.fori_loop(0, ng * PEER_HEADS // ROUTE_UNROLL, route_group, 0)

    def hidden(c):
        slabs[2 * c][...] = _dot_nt(u_ref[c * 2 * LANE:(c + 1) * 2 * LANE, :], xn_s[...])

    def accumulate(c):
        acc_s[...] += _dot(vt_ref[:, c * 2 * LANE:(c + 1) * 2 * LANE], slabs[2 * c + 1][...])

    hidden(0)
    for c in range(nb // 2):
        ht_s, a_s = slabs[2 * c], slabs[2 * c + 1]
        if c + 1 < nb // 2:
            hidden(c + 1)
        for i in range(2):
            r = slice(i * LANE, (i + 1) * LANE)
            i1 = j * nb + 2 * c + i
            for g in range(ng):
                cols = slice(g * LANE, (g + 1) * LANE)
                w = jnp.zeros((LANE, LANE), F32)
                for h in range(PEER_HEADS):
                    z = e1_s[g, h, pl.ds(i1, 1), :] * e2_s[g, h]
                    w = w + jnp.where(z >= th_s[g, h, 0:1, :], z, 0.0)
                a_s[r, cols] = (w * _gelu(ht_s[r, cols])).astype(BF16)
        if c >= 1:
            accumulate(c - 1)
    accumulate(nb // 2 - 1)

    @pl.when(j == pl.num_programs(1) - 1)
    def _finish():
        x2 = x1_ref[...] + acc_s[...].T
        y_out[...] = _rms(x2) * nfin_ref[...]


def _peer(x1, w, tm, te):
    T, D = x1.shape
    NE = w["u"].shape[0]
    assert T % tm == 0 and NE % te == 0 and tm % LANE == 0 and te % LANE == 0
    ng = tm // LANE
    consts = [w["nffn"], w["wqt"], w["keys"]]
    return pl.pallas_call(
        functools.partial(_peer_kernel, te=te),
        grid=(T // tm, NE // te),
        in_specs=[pl.BlockSpec((tm, D), lambda i, j: (i, 0))]
        + [_const_spec(c.shape) for c in consts]
        + [pl.BlockSpec((te, D), lambda i, j: (j, 0)),
           pl.BlockSpec((D, te), lambda i, j: (0, j)),
           _const_spec(w["nfin"].shape)],
        out_specs=pl.BlockSpec((tm, D), lambda i, j: (i, 0)),
        out_shape=jax.ShapeDtypeStruct((T, D), F32),
        scratch_shapes=[
            pltpu.VMEM((tm, D), BF16),
            pltpu.VMEM((PEER_HEADS * 2 * PEER_HALF, tm), BF16),
            pltpu.VMEM((ng, PEER_HEADS, N_KEYS, LANE), F32),
            pltpu.VMEM((ng, PEER_HEADS, N_KEYS, LANE), F32),
            pltpu.VMEM((ng, PEER_HEADS, 8, LANE), F32),
            pltpu.VMEM((D, tm), F32),
        ] + [pltpu.VMEM((2 * LANE, tm), F32),
             pltpu.VMEM((2 * LANE, tm), BF16)] * (te // (2 * LANE)),
        compiler_params=pltpu.CompilerParams(
            dimension_semantics=("parallel", "arbitrary"), vmem_limit_bytes=VMEM_LIMIT),
        name="peer",
    )(x1, *consts, w["u"], w["vt"], w["nfin"])


def _rope_tables(seq):
    inv = 1.0 / (ROPE_THETA ** (jnp.arange(0, QK_ROPE, 2, dtype=F32) / QK_ROPE))
    ang = jnp.arange(seq, dtype=F32)[:, None] * inv[None, :]
    cos, sin = jnp.cos(ang), jnp.sin(ang)
    pad = jnp.zeros((seq, LANE - QK_ROPE), F32)
    return (jnp.concatenate([cos, cos, pad], axis=1), jnp.concatenate([sin, sin, pad], axis=1))


def _rot_cols(wpe):
    half = QK_ROPE // 2
    return jnp.concatenate([-wpe[..., half:], wpe[..., :half]], axis=-1)


def _prep_weights(norm_mix, w_in, q_norm, kv_norm, w_uq, w_ukv, w_o_b, sgu_norm, sgu_w, sgu_b,
                  w_o_a, w_out, norm_ffn, peer_wq, peer_keys, peer_u, peer_v, norm_final):
    a, b, c = Q_LORA, Q_LORA + KV_LORA, Q_LORA + KV_LORA + QK_ROPE
    d, e, f = c + SGU_WIDTH, c + 2 * SGU_WIDTH, c + 2 * SGU_WIDTH + D_MODEL
    zpad = lambda rows, n: jnp.zeros((rows, n), F32)
    w_kr = w_in[:, b:c]
    wkr = jnp.concatenate([w_kr, zpad(D_MODEL, LANE - QK_ROPE),
                           _rot_cols(w_kr), zpad(D_MODEL, LANE - QK_ROPE)], axis=1)
    uq = w_uq.reshape(Q_LORA, MLA_HEADS, QK_NOPE + QK_ROPE)
    hpad = jnp.zeros((Q_LORA, MLA_HEADS, LANE - QK_ROPE), F32)
    wuq = jnp.concatenate([uq, hpad], axis=-1).reshape(Q_LORA, MLA_HEADS * QK_PAD)
    wuqr = jnp.concatenate([_rot_cols(uq[..., QK_NOPE:]), hpad], axis=-1).reshape(Q_LORA, MLA_HEADS * LANE)
    ukv = w_ukv.reshape(KV_LORA, MLA_HEADS, QK_NOPE + V_HEAD)
    scale = (QK_NOPE + QK_ROPE) ** -0.5 * LOG2E
    sbias = jnp.repeat(sgu_b.T, SGU_WIDTH // SGU_GROUPS, axis=1)
    row = lambda v: v.reshape(1, -1).astype(F32)
    return {
        "nmix": row(norm_mix), "wq": w_in[:, :a].astype(BF16), "wkv": w_in[:, a:b].astype(BF16),
        "wkr": wkr.astype(BF16), "wu": w_in[:, c:d].astype(BF16), "wv": w_in[:, d:e].astype(BF16),
        "wga": w_in[:, e:f].astype(BF16), "wgb": w_in[:, f:].astype(BF16),
        "qn": row(q_norm) * scale, "kvn": row(kv_norm),
        "wuq": wuq.astype(BF16), "wuqr": wuqr.astype(BF16),
        "wuk": ukv[..., :QK_NOPE].reshape(KV_LORA, MLA_HEADS * QK_NOPE).astype(BF16),
        "wuvt": ukv[..., QK_NOPE:].reshape(KV_LORA, MLA_HEADS * V_HEAD).T.astype(BF16),
        "sgn": row(sgu_norm), "sw": sgu_w.astype(BF16), "sbias": sbias.astype(F32),
        "woa": w_o_a.astype(BF16), "wob": w_o_b.astype(BF16), "wout": w_out.astype(BF16),
        "nffn": row(norm_ffn), "wqt": peer_wq.T.astype(BF16), "keys": peer_keys.astype(BF16),
        "u": peer_u.astype(BF16), "vt": peer_v.T.astype(BF16), "nfin": row(norm_final),
    }


def _trunk(x, w, front_tm=FRONT_TM, tq=ATTN_TQ, tk=ATTN_TK, post_tm=POST_TM,
           peer_tm=PEER_TM, peer_te=PEER_TE):
    B, S, D = x.shape
    cos, sin = _rope_tables(S)
    q, k, vt, ya, sgb = _front(x, cos, sin, w, front_tm)
    o = _attention(q, k, vt, tq, tk)
    flat = lambda t: t.reshape(B * S, t.shape[-1])
    x1 = _post(flat(x), flat(o), flat(ya), flat(sgb), w["wob"], w["wout"], post_tm)
    y = _peer(x1, w, peer_tm, peer_te)
    return y.reshape(B, S, D)


def kernel(x_prompt, x_sample, norm_mix, w_in, q_norm, kv_norm, w_uq, w_ukv, w_o_b, sgu_norm,
           sgu_w, sgu_b, w_o_a, w_out, norm_ffn, peer_wq, peer_keys, peer_u, peer_v, norm_final):
    assert norm_mix.shape[0] == 1, "single layer"
    w = _prep_weights(norm_mix[0], w_in[0], q_norm[0], kv_norm[0], w_uq[0], w_ukv[0], w_o_b[0],
                      sgu_norm[0], sgu_w[0], sgu_b[0], w_o_a[0], w_out[0], norm_ffn[0],
                      peer_wq[0], peer_keys[0], peer_u[0], peer_v[0], norm_final)
    return (_trunk(x_prompt, w), _trunk(x_sample, w))
```

```python
import functools

import jax
import jax.numpy as jnp
from jax import lax
from jax.experimental import pallas as pl
from jax.experimental.pallas import tpu as pltpu

F32 = jnp.float32
BF16 = jnp.bfloat16

D_MODEL = 1024
SGU_CHUNK = 128
SGU_GROUPS = 8
SGU_WIDTH = 1024
MLA_HEADS = 8
QK_NOPE = 128
QK_ROPE = 64
V_HEAD = 128
Q_LORA = 384
KV_LORA = 256
ROPE_THETA = 10000.0
PEER_HEADS = 8
N_KEYS = 128
PEER_HALF = 128
PEER_TOPK = 16
EPS = 1e-6

LANE = 128
BF16_ROWS = 16
QK_PAD = 256
LOG2E = 1.4426950408889634
VMEM_LIMIT = 56 * 1024 * 1024

FRONT_TM = 512
ATTN_TQ = 512
ATTN_TK = 1024
ATTN_UNROLL = 4
POST_TM = 512
PEER_TM = 512
PEER_TE = 2048
ROUTE_UNROLL = 4


def _rms(x, eps=EPS):
    return x * lax.rsqrt(jnp.mean(x * x, axis=-1, keepdims=True) + eps)


_GELU_K0 = -2.0 * 0.7978845608028654 * LOG2E
_GELU_K1 = _GELU_K0 * 0.044715


def _gelu(x):
    return x / (1.0 + jnp.exp2(x * (_GELU_K0 + _GELU_K1 * (x * x))))


def _dot(a, b):
    return jnp.dot(a, b, preferred_element_type=F32)


def _dot_nt(a, b):
    return lax.dot_general(a, b, (((1,), (1,)), ((), ())), preferred_element_type=F32)


def _const_spec(shape):
    nd = len(shape)
    return pl.BlockSpec(shape, lambda *_: (0,) * nd, pipeline_mode=pl.Buffered(1))


def _front_kernel(x_ref, cos_ref, sin_ref, nmix_ref, wq_ref, wkv_ref, wkr_ref, wu_ref, wv_ref,
                  wga_ref, wgb_ref, qn_ref, kvn_ref, wuq_ref, wuqr_ref, wuk_ref, wuvt_ref,
                  sgn_ref, sw_ref, sbias_ref, woa_ref,
                  q_out, k_out, vt_out, ya_out, sgb_out, vn_s, sgu_s):
    tm = x_ref.shape[1]
    x = x_ref[0]
    xn = (_rms(x) * nmix_ref[...]).astype(BF16)
    cos = cos_ref[...]
    sin = sin_ref[...]

    c_q = (_rms(_dot(xn, wq_ref[...])) * qn_ref[...]).astype(BF16)
    qa = _dot(c_q, wuq_ref[...])
    qr = _dot(c_q, wuqr_ref[...])
    for h in range(MLA_HEADS):
        q_out[0, :, h * QK_PAD:h * QK_PAD + LANE] = qa[:, h * QK_PAD:h * QK_PAD + LANE].astype(BF16)
        pe = qa[:, h * QK_PAD + LANE:(h + 1) * QK_PAD] * cos + qr[:, h * LANE:(h + 1) * LANE] * sin
        q_out[0, :, h * QK_PAD + LANE:(h + 1) * QK_PAD] = pe.astype(BF16)

    c_kv = (_rms(_dot(xn, wkv_ref[...])) * kvn_ref[...]).astype(BF16)
    kn = _dot(c_kv, wuk_ref[...])
    kr = _dot(xn, wkr_ref[...])
    kpe = (kr[:, :LANE] * cos + kr[:, LANE:] * sin).astype(BF16)
    for h in range(MLA_HEADS):
        k_out[0, :, h * QK_PAD:h * QK_PAD + LANE] = kn[:, h * LANE:(h + 1) * LANE].astype(BF16)
        k_out[0, :, h * QK_PAD + LANE:(h + 1) * QK_PAD] = kpe
    vt_out[0] = _dot_nt(wuvt_ref[...], c_kv).astype(BF16)

    v = _gelu(_dot(xn, wv_ref[...]))
    vn_s[...] = (_rms(v) * sgn_ref[...]).astype(BF16)
    u = _gelu(_dot(xn, wu_ref[...]))
    for c in range(tm // SGU_CHUNK):
        rows = slice(c * SGU_CHUNK, (c + 1) * SGU_CHUNK)
        for g in range(SGU_GROUPS):
            cols = slice(g * LANE, (g + 1) * LANE)
            mixed = _dot(sw_ref[g], vn_s[rows, cols]) + sbias_ref[:, cols]
            sgu_s[rows, cols] = (u[rows, cols] * mixed).astype(BF16)
    ya = _dot(sgu_s[...], woa_ref[...])
    ya_out[0] = (jax.nn.sigmoid(_dot(xn, wga_ref[...])) * ya).astype(BF16)
    sgb_out[0] = jax.nn.sigmoid(_dot(xn, wgb_ref[...])).astype(BF16)


def _front(x, cos, sin, w, tm):
    B, S, D = x.shape
    assert S % tm == 0 and tm % SGU_CHUNK == 0
    nt = S // tm
    tok = lambda width: pl.BlockSpec((1, tm, width), lambda b, i: (b, i, 0))
    pos = pl.BlockSpec((tm, LANE), lambda b, i: (i, 0))
    consts = [w["nmix"], w["wq"], w["wkv"], w["wkr"], w["wu"], w["wv"], w["wga"], w["wgb"],
              w["qn"], w["kvn"], w["wuq"], w["wuqr"], w["wuk"], w["wuvt"], w["sgn"], w["sw"],
              w["sbias"], w["woa"]]
    out_shape = (
        jax.ShapeDtypeStruct((B, S, MLA_HEADS * QK_PAD), BF16),
        jax.ShapeDtypeStruct((B, S, MLA_HEADS * QK_PAD), BF16),
        jax.ShapeDtypeStruct((B, MLA_HEADS * V_HEAD, S), BF16),
        jax.ShapeDtypeStruct((B, S, D), BF16),
        jax.ShapeDtypeStruct((B, S, D), BF16),
    )
    return pl.pallas_call(
        _front_kernel,
        grid=(B, nt),
        in_specs=[tok(D), pos, pos] + [_const_spec(c.shape) for c in consts],
        out_specs=(tok(MLA_HEADS * QK_PAD), tok(MLA_HEADS * QK_PAD),
                   pl.BlockSpec((1, MLA_HEADS * V_HEAD, tm), lambda b, i: (b, 0, i)),
                   tok(D), tok(D)),
        out_shape=out_shape,
        scratch_shapes=[pltpu.VMEM((tm, SGU_WIDTH), BF16), pltpu.VMEM((tm, SGU_WIDTH), BF16)],
        compiler_params=pltpu.CompilerParams(
            dimension_semantics=("parallel", "parallel"), vmem_limit_bytes=VMEM_LIMIT),
        name="front",
    )(x, cos, sin, *consts)


def _attn_kernel(q_ref, k_ref, vt_ref, o_ref, sa_s, sb_s, *, tk):
    tq = q_ref.shape[1]
    nk = k_ref.shape[1] // tk
    q = q_ref[0]

    def scores(s_ref, c):
        start = pl.multiple_of(c * tk, tk)
        s = _dot_nt(k_ref[0, pl.ds(start, tk), :], q)
        s_ref[...] = s
        return jnp.max(s, axis=0, keepdims=True)

    def update(s_ref, mc, c, carry):
        m, l, acc = carry
        start = pl.multiple_of(c * tk, tk)
        m_new = jnp.maximum(m, mc)
        alpha = jnp.exp2(m - m_new)
        p = jnp.exp2(s_ref[...] - m_new)
        l = alpha * l + jnp.sum(p, axis=0, keepdims=True)
        acc = alpha * acc + _dot(vt_ref[0, :, pl.ds(start, tk)], p.astype(BF16))
        return m_new, l, acc

    def run(c0, n, mc, state, more):
        bufs = (sa_s, sb_s)
        for t in range(n):
            mc_next = scores(bufs[(t + 1) % 2], c0 + t + 1) if (t + 1 < n or more) else None
            state = update(bufs[t % 2], mc, c0 + t, state)
            mc = mc_next
        return mc, state

    def body(i, carry):
        return run(i * ATTN_UNROLL, ATTN_UNROLL, *carry, more=True)

    state = (jnp.full((1, tq), -jnp.inf, F32), jnp.zeros((1, tq), F32), jnp.zeros((V_HEAD, tq), F32))
    nloop = nk // ATTN_UNROLL - 1
    mc, state = lax.fori_loop(0, nloop, body, (scores(sa_s, 0), state))
    _, (_, l, acc) = run(nloop * ATTN_UNROLL, ATTN_UNROLL, mc, state, more=False)
    o_ref[0] = (acc / l).T.astype(o_ref.dtype)


def _attention(q, k, vt, tq, tk):
    B, S, _ = q.shape
    assert S % tq == 0 and S % (ATTN_UNROLL * tk) == 0 and ATTN_UNROLL % 2 == 0
    return pl.pallas_call(
        functools.partial(_attn_kernel, tk=tk),
        scratch_shapes=[pltpu.VMEM((tk, tq), F32)] * 2,
        grid=(B, MLA_HEADS, S // tq),
        in_specs=[
            pl.BlockSpec((1, tq, QK_PAD), lambda b, h, i: (b, i, h)),
            pl.BlockSpec((1, S, QK_PAD), lambda b, h, i: (b, 0, h)),
            pl.BlockSpec((1, V_HEAD, S), lambda b, h, i: (b, h, 0)),
        ],
        out_specs=pl.BlockSpec((1, tq, V_HEAD), lambda b, h, i: (b, i, h)),
        out_shape=jax.ShapeDtypeStruct((B, S, MLA_HEADS * V_HEAD), BF16),
        compiler_params=pltpu.CompilerParams(
            dimension_semantics=("parallel", "parallel", "arbitrary"),
            vmem_limit_bytes=VMEM_LIMIT),
        name="attn",
    )(q, k, vt)


def _post_kernel(x_ref, o_ref, ya_ref, sgb_ref, wob_ref, wout_ref, x1_out):
    yb = _dot(o_ref[...], wob_ref[...])
    merged = ya_ref[...].astype(F32) + sgb_ref[...].astype(F32) * yb
    x1_out[...] = x_ref[...] + _dot(merged.astype(BF16), wout_ref[...])


def _post(x2, o2, ya2, sgb2, wob, wout, tm):
    T, D = x2.shape
    assert T % tm == 0
    tok = pl.BlockSpec((tm, D), lambda i: (i, 0))
    return pl.pallas_call(
        _post_kernel,
        grid=(T // tm,),
        in_specs=[tok, tok, tok, tok, _const_spec(wob.shape), _const_spec(wout.shape)],
        out_specs=tok,
        out_shape=jax.ShapeDtypeStruct((T, D), F32),
        compiler_params=pltpu.CompilerParams(
            dimension_semantics=("parallel",), vmem_limit_bytes=VMEM_LIMIT),
        name="post",
    )(x2, o2, ya2, sgb2, wob, wout)


def _col_max(a):
    return jnp.max(a, axis=0, keepdims=True)


def _sorting_network(n):
    comps = []
    p = 1
    while p < n:
        k = p
        while k >= 1:
            for j in range(k % p, n - k, 2 * k):
                for i in range(min(k, n - j - k)):
                    if (i + j) // (2 * p) == (i + j + k) // (2 * p):
                        comps.append((i + j, i + j + k))
            k //= 2
        p *= 2
    return comps


def _merge_heads(lists, extra, n):
    lists = list(lists)
    vals = []
    for t in range(n):
        head = lists[0] if extra is None else jnp.maximum(lists[0], extra)
        m = jnp.maximum(_col_max(head), 0.0)
        vals.append(m)
        if t == n - 1:
            break
        hit = lists[0] == m
        live = min(len(lists) - 1, n - 1 - t)
        for d in range(live):
            lists[d] = jnp.where(hit, lists[d + 1], lists[d])
        if live == len(lists) - 1:
            lists[live] = jnp.where(hit, -1.0, lists[live])
        if extra is not None:
            extra = jnp.where(extra == m, -1.0, extra)
    return vals


def _top_values(e, n):
    groups = [e[8 * r:8 * r + 8] for r in range(e.shape[0] // 8)]
    for a, b in _sorting_network(len(groups)):
        groups[a], groups[b] = jnp.maximum(groups[a], groups[b]), jnp.minimum(groups[a], groups[b])
    return jnp.concatenate(_merge_heads(groups[:n], None, n), axis=0)


def _peer_kernel(x1_ref, nffn_ref, wqt_ref, keys_ref, u_ref, vt_ref, nfin_ref, y_out,
                 xn_s, qt_s, e1_s, e2_s, th_s, acc_s, *slabs, te):
    tm = x1_ref.shape[0]
    ng = tm // LANE
    nb = te // LANE
    j = pl.program_id(1)

    @pl.when(j == 0)
    def _route():
        xn = (_rms(x1_ref[...]) * nffn_ref[...]).astype(BF16)
        xn_s[...] = xn
        qt_s[...] = _dot_nt(wqt_ref[...], xn).astype(BF16)
        acc_s[...] = jnp.zeros_like(acc_s)
        for h in range(PEER_HEADS):
            for p, e_s in ((0, e1_s), (1, e2_s)):
                r0 = (2 * h + p) * PEER_HALF
                s = _dot(keys_ref[p], qt_s[r0:r0 + PEER_HALF, :])
                e = jnp.exp(s - _col_max(s))
                for g in range(ng):
                    e_s[g, h] = e[:, g * LANE:(g + 1) * LANE]

        def route_one(g, h):
            e1 = e1_s[g, h]
            e2 = e2_s[g, h]
            v1 = _top_values(e1, PEER_TOPK)
            v2 = _top_values(e2, PEER_TOPK)
            half = PEER_TOPK // 2

            def products(w2):
                return ([v1[:half] * w2[b:b + 1] for b in range(PEER_TOPK)], v1[half:] * w2[0:1])

            cand = products(v2)
            th = _merge_heads(*cand, PEER_TOPK)[-1]
            sel = ([c >= th for c in cand[0]], cand[1] >= th)

            def selected(prod, fill, combine):
                out = jnp.where(sel[1], prod[1], fill)
                for s, c in zip(sel[0], prod[0]):
                    out = combine(out, jnp.where(s, c, fill))
                return out

            inv_z = 1.0 / jnp.sum(selected(cand, 0.0, jnp.add), axis=0, keepdims=True)
            e2_s[g, h] = e2 * inv_z
            thn = jnp.min(selected(products(v2 * inv_z), jnp.inf, jnp.minimum), axis=0, keepdims=True)
            th_s[g, h] = jnp.broadcast_to(thn, (8, LANE))

        def route_group(i, _):
            for k in range(ROUTE_UNROLL):
                item = ROUTE_UNROLL * i + k
                route_one(item // PEER_HEADS, item % PEER_HEADS)
            return 0

        lax.fori_loop(0, ng * PEER_HEADS // ROUTE_UNROLL, route_group, 0)

    def hidden(c):
        slabs[2 * c][...] = _dot_nt(u_ref[c * 2 * LANE:(c + 1) * 2 * LANE, :], xn_s[...])

    def accumulate(c):
        acc_s[...] += _dot(vt_ref[:, c * 2 * LANE:(c + 1) * 2 * LANE], slabs[2 * c + 1][...])

    hidden(0)
    for c in range(nb // 2):
        ht_s, a_s = slabs[2 * c], slabs[2 * c + 1]
        if c + 1 < nb // 2:
            hidden(c + 1)
        for i in range(2):
            r = slice(i * LANE, (i + 1) * LANE)
            i1 = j * nb + 2 * c + i
            for g in range(ng):
                cols = slice(g * LANE, (g + 1) * LANE)
                w = jnp.zeros((LANE, LANE), F32)
                for h in range(PEER_HEADS):
                    z = e1_s[g, h, pl.ds(i1, 1), :] * e2_s[g, h]
                    w = w + jnp.where(z >= th_s[g, h, 0:1, :], z, 0.0)
                a_s[r, cols] = (w * _gelu(ht_s[r, cols])).astype(BF16)
        if c >= 1:
            accumulate(c - 1)
    accumulate(nb // 2 - 1)

    @pl.when(j == pl.num_programs(1) - 1)
    def _finish():
        x2 = x1_ref[...] + acc_s[...].T
        y_out[...] = _rms(x2) * nfin_ref[...]


def _peer(x1, w, tm, te):
    T, D = x1.shape
    NE = w["u"].shape[0]
    assert T % tm == 0 and NE % te == 0 and tm % LANE == 0 and te % LANE == 0
    ng = tm // LANE
    consts = [w["nffn"], w["wqt"], w["keys"]]
    return pl.pallas_call(
        functools.partial(_peer_kernel, te=te),
        grid=(T // tm, NE // te),
        in_specs=[pl.BlockSpec((tm, D), lambda i, j: (i, 0))]
        + [_const_spec(c.shape) for c in consts]
        + [pl.BlockSpec((te, D), lambda i, j: (j, 0)),
           pl.BlockSpec((D, te), lambda i, j: (0, j)),
           _const_spec(w["nfin"].shape)],
        out_specs=pl.BlockSpec((tm, D), lambda i, j: (i, 0)),
        out_shape=jax.ShapeDtypeStruct((T, D), F32),
        scratch_shapes=[
            pltpu.VMEM((tm, D), BF16),
            pltpu.VMEM((PEER_HEADS * 2 * PEER_HALF, tm), BF16),
            pltpu.VMEM((ng, PEER_HEADS, N_KEYS, LANE), F32),
            pltpu.VMEM((ng, PEER_HEADS, N_KEYS, LANE), F32),
            pltpu.VMEM((ng, PEER_HEADS, 8, LANE), F32),
            pltpu.VMEM((D, tm), F32),
        ] + [pltpu.VMEM((2 * LANE, tm), F32),
             pltpu.VMEM((2 * LANE, tm), BF16)] * (te // (2 * LANE)),
        compiler_params=pltpu.CompilerParams(
            dimension_semantics=("parallel", "arbitrary"), vmem_limit_bytes=VMEM_LIMIT),
        name="peer",
    )(x1, *consts, w["u"], w["vt"], w["nfin"])


def _rope_tables(seq):
    inv = 1.0 / (ROPE_THETA ** (jnp.arange(0, QK_ROPE, 2, dtype=F32) / QK_ROPE))
    ang = jnp.arange(seq, dtype=F32)[:, None] * inv[None, :]
    cos, sin = jnp.cos(ang), jnp.sin(ang)
    pad = jnp.zeros((seq, LANE - QK_ROPE), F32)
    return (jnp.concatenate([cos, cos, pad], axis=1), jnp.concatenate([sin, sin, pad], axis=1))


def _rot_cols(wpe):
    half = QK_ROPE // 2
    return jnp.concatenate([-wpe[..., half:], wpe[..., :half]], axis=-1)


def _prep_weights(norm_mix, w_in, q_norm, kv_norm, w_uq, w_ukv, w_o_b, sgu_norm, sgu_w, sgu_b,
                  w_o_a, w_out, norm_ffn, peer_wq, peer_keys, peer_u, peer_v, norm_final):
    a, b, c = Q_LORA, Q_LORA + KV_LORA, Q_LORA + KV_LORA + QK_ROPE
    d, e, f = c + SGU_WIDTH, c + 2 * SGU_WIDTH, c + 2 * SGU_WIDTH + D_MODEL
    zpad = lambda rows, n: jnp.zeros((rows, n), F32)
    w_kr = w_in[:, b:c]
    wkr = jnp.concatenate([w_kr, zpad(D_MODEL, LANE - QK_ROPE),
                           _rot_cols(w_kr), zpad(D_MODEL, LANE - QK_ROPE)], axis=1)
    uq = w_uq.reshape(Q_LORA, MLA_HEADS, QK_NOPE + QK_ROPE)
    hpad = jnp.zeros((Q_LORA, MLA_HEADS, LANE - QK_ROPE), F32)
    wuq = jnp.concatenate([uq, hpad], axis=-1).reshape(Q_LORA, MLA_HEADS * QK_PAD)
    wuqr = jnp.concatenate([_rot_cols(uq[..., QK_NOPE:]), hpad], axis=-1).reshape(Q_LORA, MLA_HEADS * LANE)
    ukv = w_ukv.reshape(KV_LORA, MLA_HEADS, QK_NOPE + V_HEAD)
    scale = (QK_NOPE + QK_ROPE) ** -0.5 * LOG2E
    sbias = jnp.repeat(sgu_b.T, SGU_WIDTH // SGU_GROUPS, axis=1)
    row = lambda v: v.reshape(1, -1).astype(F32)
    return {
        "nmix": row(norm_mix), "wq": w_in[:, :a].astype(BF16), "wkv": w_in[:, a:b].astype(BF16),
        "wkr": wkr.astype(BF16), "wu": w_in[:, c:d].astype(BF16), "wv": w_in[:, d:e].astype(BF16),
        "wga": w_in[:, e:f].astype(BF16), "wgb": w_in[:, f:].astype(BF16),
        "qn": row(q_norm) * scale, "kvn": row(kv_norm),
        "wuq": wuq.astype(BF16), "wuqr": wuqr.astype(BF16),
        "wuk": ukv[..., :QK_NOPE].reshape(KV_LORA, MLA_HEADS * QK_NOPE).astype(BF16),
        "wuvt": ukv[..., QK_NOPE:].reshape(KV_LORA, MLA_HEADS * V_HEAD).T.astype(BF16),
        "sgn": row(sgu_norm), "sw": sgu_w.astype(BF16), "sbias": sbias.astype(F32),
        "woa": w_o_a.astype(BF16), "wob": w_o_b.astype(BF16), "wout": w_out.astype(BF16),
        "nffn": row(norm_ffn), "wqt": peer_wq.T.astype(BF16), "keys": peer_keys.astype(BF16),
        "u": peer_u.astype(BF16), "vt": peer_v.T.astype(BF16), "nfin": row(norm_final),
    }


def _trunk(x, w, front_tm=FRONT_TM, tq=ATTN_TQ, tk=ATTN_TK, post_tm=POST_TM,
           peer_tm=PEER_TM, peer_te=PEER_TE):
    B, S, D = x.shape
    cos, sin = _rope_tables(S)
    q, k, vt, ya, sgb = _front(x, cos, sin, w, front_tm)
    o = _attention(q, k, vt, tq, tk)
    flat = lambda t: t.reshape(B * S, t.shape[-1])
    x1 = _post(flat(x), flat(o), flat(ya), flat(sgb), w["wob"], w["wout"], post_tm)
    y = _peer(x1, w, peer_tm, peer_te)
    return y.reshape(B, S, D)


def kernel(x_prompt, x_sample, norm_mix, w_in, q_norm, kv_norm, w_uq, w_ukv, w_o_b, sgu_norm,
           sgu_w, sgu_b, w_o_a, w_out, norm_ffn, peer_wq, peer_keys, peer_u, peer_v, norm_final):
    assert norm_mix.shape[0] == 1, "single layer"
    w = _prep_weights(norm_mix[0], w_in[0], q_norm[0], kv_norm[0], w_uq[0], w_ukv[0], w_o_b[0],
                      sgu_norm[0], sgu_w[0], sgu_b[0], w_o_a[0], w_out[0], norm_ffn[0],
                      peer_wq[0], peer_keys[0], peer_u[0], peer_v[0], norm_final)
    return (_trunk(x_prompt, w), _trunk(x_sample, w))
```

```python
import functools

import jax
import jax.numpy as jnp
from jax import lax
from jax.experimental import pallas as pl
from jax.experimental.pallas import tpu as pltpu

F32 = jnp.float32
BF16 = jnp.bfloat16

D_MODEL = 1024
SGU_CHUNK = 128
SGU_GROUPS = 8
SGU_WIDTH = 1024
MLA_HEADS = 8
QK_NOPE = 128
QK_ROPE = 64
V_HEAD = 128
Q_LORA = 384
KV_LORA = 256
ROPE_THETA = 10000.0
PEER_HEADS = 8
N_KEYS = 128
PEER_HALF = 128
PEER_TOPK = 16
EPS = 1e-6

LANE = 128
BF16_ROWS = 16
QK_PAD = 256
LOG2E = 1.4426950408889634
VMEM_LIMIT = 56 * 1024 * 1024

FRONT_TM = 512
ATTN_TQ = 1024
ATTN_TK = 1024
ATTN_UNROLL = 4
POST_TM = 1024
PEER_TM = 512
PEER_TE = 2048
ROUTE_UNROLL = 4


def _rms(x, eps=EPS):
    return x * lax.rsqrt(jnp.mean(x * x, axis=-1, keepdims=True) + eps)


_GELU_K0 = -2.0 * 0.7978845608028654 * LOG2E
_GELU_K1 = _GELU_K0 * 0.044715


def _gelu(x):
    return x / (1.0 + jnp.exp2(x * (_GELU_K0 + _GELU_K1 * (x * x))))


def _dot(a, b):
    return jnp.dot(a, b, preferred_element_type=F32)


def _dot_nt(a, b):
    return lax.dot_general(a, b, (((1,), (1,)), ((), ())), preferred_element_type=F32)


def _const_spec(shape):
    nd = len(shape)
    return pl.BlockSpec(shape, lambda *_: (0,) * nd, pipeline_mode=pl.Buffered(1))


def _front_kernel(x_ref, cos_ref, sin_ref, nmix_ref, wq_ref, wkv_ref, wkr_ref, wu_ref, wv_ref,
                  wga_ref, wgb_ref, qn_ref, kvn_ref, wuq_ref, wuqr_ref, wuk_ref, wuvt_ref,
                  sgn_ref, sw_ref, sbias_ref, woa_ref,
                  q_out, k_out, vt_out, ya_out, sgb_out, vn_s, sgu_s):
    tm = x_ref.shape[1]
    x = x_ref[0]
    xn = (_rms(x) * nmix_ref[...]).astype(BF16)
    cos = cos_ref[...]
    sin = sin_ref[...]

    c_q = (_rms(_dot(xn, wq_ref[...])) * qn_ref[...]).astype(BF16)
    qa = _dot(c_q, wuq_ref[...])
    qr = _dot(c_q, wuqr_ref[...])
    for h in range(MLA_HEADS):
        q_out[0, :, h * QK_PAD:h * QK_PAD + LANE] = qa[:, h * QK_PAD:h * QK_PAD + LANE].astype(BF16)
        pe = qa[:, h * QK_PAD + LANE:(h + 1) * QK_PAD] * cos + qr[:, h * LANE:(h + 1) * LANE] * sin
        q_out[0, :, h * QK_PAD + LANE:(h + 1) * QK_PAD] = pe.astype(BF16)

    c_kv = (_rms(_dot(xn, wkv_ref[...])) * kvn_ref[...]).astype(BF16)
    kn = _dot(c_kv, wuk_ref[...])
    kr = _dot(xn, wkr_ref[...])
    kpe = (kr[:, :LANE] * cos + kr[:, LANE:] * sin).astype(BF16)
    for h in range(MLA_HEADS):
        k_out[0, :, h * QK_PAD:h * QK_PAD + LANE] = kn[:, h * LANE:(h + 1) * LANE].astype(BF16)
        k_out[0, :, h * QK_PAD + LANE:(h + 1) * QK_PAD] = kpe
    vt_out[0] = _dot_nt(wuvt_ref[...], c_kv).astype(BF16)

    v = _gelu(_dot(xn, wv_ref[...]))
    vn_s[...] = (_rms(v) * sgn_ref[...]).astype(BF16)
    u = _gelu(_dot(xn, wu_ref[...]))
    for c in range(tm // SGU_CHUNK):
        rows = slice(c * SGU_CHUNK, (c + 1) * SGU_CHUNK)
        for g in range(SGU_GROUPS):
            cols = slice(g * LANE, (g + 1) * LANE)
            mixed = _dot(sw_ref[g], vn_s[rows, cols]) + sbias_ref[:, cols]
            sgu_s[rows, cols] = (u[rows, cols] * mixed).astype(BF16)
    ya = _dot(sgu_s[...], woa_ref[...])
    ya_out[0] = (jax.nn.sigmoid(_dot(xn, wga_ref[...])) * ya).astype(BF16)
    sgb_out[0] = jax.nn.sigmoid(_dot(xn, wgb_ref[...])).astype(BF16)


def _front(x, cos, sin, w, tm):
    B, S, D = x.shape
    assert S % tm == 0 and tm % SGU_CHUNK == 0
    nt = S // tm
    tok = lambda width: pl.BlockSpec((1, tm, width), lambda b, i: (b, i, 0))
    pos = pl.BlockSpec((tm, LANE), lambda b, i: (i, 0))
    consts = [w["nmix"], w["wq"], w["wkv"], w["wkr"], w["wu"], w["wv"], w["wga"], w["wgb"],
              w["qn"], w["kvn"], w["wuq"], w["wuqr"], w["wuk"], w["wuvt"], w["sgn"], w["sw"],
              w["sbias"], w["woa"]]
    out_shape = (
        jax.ShapeDtypeStruct((B, S, MLA_HEADS * QK_PAD), BF16),
        jax.ShapeDtypeStruct((B, S, MLA_HEADS * QK_PAD), BF16),
        jax.ShapeDtypeStruct((B, MLA_HEADS * V_HEAD, S), BF16),
        jax.ShapeDtypeStruct((B, S, D), BF16),
        jax.ShapeDtypeStruct((B, S, D), BF16),
    )
    return pl.pallas_call(
        _front_kernel,
        grid=(B, nt),
        in_specs=[tok(D), pos, pos] + [_const_spec(c.shape) for c in consts],
        out_specs=(tok(MLA_HEADS * QK_PAD), tok(MLA_HEADS * QK_PAD),
                   pl.BlockSpec((1, MLA_HEADS * V_HEAD, tm), lambda b, i: (b, 0, i)),
                   tok(D), tok(D)),
        out_shape=out_shape,
        scratch_shapes=[pltpu.VMEM((tm, SGU_WIDTH), BF16), pltpu.VMEM((tm, SGU_WIDTH), BF16)],
        compiler_params=pltpu.CompilerParams(
            dimension_semantics=("parallel", "parallel"), vmem_limit_bytes=VMEM_LIMIT),
        name="front",
    )(x, cos, sin, *consts)


def _attn_kernel(q_ref, k_ref, vt_ref, o_ref, sa_s, sb_s, *, tk):
    tq = q_ref.shape[1]
    nk = k_ref.shape[1] // tk
    q = q_ref[0]

    def scores(s_ref, c):
        start = pl.multiple_of(c * tk, tk)
        s = _dot_nt(k_ref[0, pl.ds(start, tk), :], q)
        s_ref[...] = s
        return jnp.max(s, axis=0, keepdims=True)

    def update(s_ref, mc, c, carry):
        m, l, acc = carry
        start = pl.multiple_of(c * tk, tk)
        m_new = jnp.maximum(m, mc)
        alpha = jnp.exp2(m - m_new)
        p = jnp.exp2(s_ref[...] - m_new)
        l = alpha * l + jnp.sum(p, axis=0, keepdims=True)
        acc = alpha * acc + _dot(vt_ref[0, :, pl.ds(start, tk)], p.astype(BF16))
        return m_new, l, acc

    def run(c0, n, mc, state, more):
        bufs = (sa_s, sb_s)
        for t in range(n):
            mc_next = scores(bufs[(t + 1) % 2], c0 + t + 1) if (t + 1 < n or more) else None
            state = update(bufs[t % 2], mc, c0 + t, state)
            mc = mc_next
        return mc, state

    def body(i, carry):
        return run(i * ATTN_UNROLL, ATTN_UNROLL, *carry, more=True)

    state = (jnp.full((1, tq), -jnp.inf, F32), jnp.zeros((1, tq), F32), jnp.zeros((V_HEAD, tq), F32))
    nloop = nk // ATTN_UNROLL - 1
    mc, state = lax.fori_loop(0, nloop, body, (scores(sa_s, 0), state))
    _, (_, l, acc) = run(nloop * ATTN_UNROLL, ATTN_UNROLL, mc, state, more=False)
    o_ref[0] = (acc / l).T.astype(o_ref.dtype)


def _attention(q, k, vt, tq, tk):
    B, S, _ = q.shape
    assert S % tq == 0 and S % (ATTN_UNROLL * tk) == 0 and ATTN_UNROLL % 2 == 0
    return pl.pallas_call(
        functools.partial(_attn_kernel, tk=tk),
        scratch_shapes=[pltpu.VMEM((tk, tq), F32)] * 2,
        grid=(B, MLA_HEADS, S // tq),
        in_specs=[
            pl.BlockSpec((1, tq, QK_PAD), lambda b, h, i: (b, i, h)),
            pl.BlockSpec((1, S, QK_PAD), lambda b, h, i: (b, 0, h)),
            pl.BlockSpec((1, V_HEAD, S), lambda b, h, i: (b, h, 0)),
        ],
        out_specs=pl.BlockSpec((1, tq, V_HEAD), lambda b, h, i: (b, i, h)),
        out_shape=jax.ShapeDtypeStruct((B, S, MLA_HEADS * V_HEAD), BF16),
        compiler_params=pltpu.CompilerParams(
            dimension_semantics=("parallel", "parallel", "arbitrary"),
            vmem_limit_bytes=VMEM_LIMIT),
        name="attn",
    )(q, k, vt)


def _post_kernel(x_ref, o_ref, ya_ref, sgb_ref, wob_ref, wout_ref, x1_out):
    yb = _dot(o_ref[...], wob_ref[...])
    merged = ya_ref[...].astype(F32) + sgb_ref[...].astype(F32) * yb
    x1_out[...] = x_ref[...] + _dot(merged.astype(BF16), wout_ref[...])


def _post(x2, o2, ya2, sgb2, wob, wout, tm):
    T, D = x2.shape
    assert T % tm == 0
    tok = pl.BlockSpec((tm, D), lambda i: (i, 0))
    return pl.pallas_call(
        _post_kernel,
        grid=(T // tm,),
        in_specs=[tok, tok, tok, tok, _const_spec(wob.shape), _const_spec(wout.shape)],
        out_specs=tok,
        out_shape=jax.ShapeDtypeStruct((T, D), F32),
        compiler_params=pltpu.CompilerParams(
            dimension_semantics=("parallel",), vmem_limit_bytes=VMEM_LIMIT),
        name="post",
    )(x2, o2, ya2, sgb2, wob, wout)


def _col_max(a):
    return jnp.max(a, axis=0, keepdims=True)


def _sorting_network(n):
    comps = []
    p = 1
    while p < n:
        k = p
        while k >= 1:
            for j in range(k % p, n - k, 2 * k):
                for i in range(min(k, n - j - k)):
                    if (i + j) // (2 * p) == (i + j + k) // (2 * p):
                        comps.append((i + j, i + j + k))
            k //= 2
        p *= 2
    return comps


def _merge_heads(lists, extra, n):
    lists = list(lists)
    vals = []
    for t in range(n):
        head = lists[0] if extra is None else jnp.maximum(lists[0], extra)
        m = jnp.maximum(_col_max(head), 0.0)
        vals.append(m)
        if t == n - 1:
            break
        hit = lists[0] == m
        live = min(len(lists) - 1, n - 1 - t)
        for d in range(live):
            lists[d] = jnp.where(hit, lists[d + 1], lists[d])
        if live == len(lists) - 1:
            lists[live] = jnp.where(hit, -1.0, lists[live])
        if extra is not None:
            extra = jnp.where(extra == m, -1.0, extra)
    return vals


def _top_values(e, n):
    groups = [e[8 * r:8 * r + 8] for r in range(e.shape[0] // 8)]
    for a, b in _sorting_network(len(groups)):
        groups[a], groups[b] = jnp.maximum(groups[a], groups[b]), jnp.minimum(groups[a], groups[b])
    return jnp.concatenate(_merge_heads(groups[:n], None, n), axis=0)


def _peer_kernel(x1_ref, nffn_ref, wqt_ref, keys_ref, u_ref, vt_ref, nfin_ref, y_out,
                 xn_s, qt_s, e1_s, e2_s, th_s, acc_s, *slabs, te):
    tm = x1_ref.shape[0]
    ng = tm // LANE
    nb = te // LANE
    j = pl.program_id(1)

    @pl.when(j == 0)
    def _route():
        xn = (_rms(x1_ref[...]) * nffn_ref[...]).astype(BF16)
        xn_s[...] = xn
        qt_s[...] = _dot_nt(wqt_ref[...], xn).astype(BF16)
        acc_s[...] = jnp.zeros_like(acc_s)
        for h in range(PEER_HEADS):
            for p, e_s in ((0, e1_s), (1, e2_s)):
                r0 = (2 * h + p) * PEER_HALF
                s = _dot(keys_ref[p], qt_s[r0:r0 + PEER_HALF, :])
                e = jnp.exp(s - _col_max(s))
                for g in range(ng):
                    e_s[g, h] = e[:, g * LANE:(g + 1) * LANE]

        def route_one(g, h):
            e1 = e1_s[g, h]
            e2 = e2_s[g, h]
            v1 = _top_values(e1, PEER_TOPK)
            v2 = _top_values(e2, PEER_TOPK)
            half = PEER_TOPK // 2

            def products(w2):
                return ([v1[:half] * w2[b:b + 1] for b in range(PEER_TOPK)], v1[half:] * w2[0:1])

            cand = products(v2)
            th = _merge_heads(*cand, PEER_TOPK)[-1]
            sel = ([c >= th for c in cand[0]], cand[1] >= th)

            def selected(prod, fill, combine):
                out = jnp.where(sel[1], prod[1], fill)
                for s, c in zip(sel[0], prod[0]):
                    out = combine(out, jnp.where(s, c, fill))
                return out

            inv_z = 1.0 / jnp.sum(selected(cand, 0.0, jnp.add), axis=0, keepdims=True)
            e2_s[g, h] = e2 * inv_z
            thn = jnp.min(selected(products(v2 * inv_z), jnp.inf, jnp.minimum), axis=0, keepdims=True)
            th_s[g, h] = jnp.broadcast_to(thn, (8, LANE))

        def route_group(i, _):
            for k in range(ROUTE_UNROLL):
                item = ROUTE_UNROLL * i + k
                route_one(item // PEER_HEADS, item % PEER_HEADS)
            return 0

        lax.fori_loop(0, ng * PEER_HEADS // ROUTE_UNROLL, route_group, 0)

    def hidden(c):
        slabs[2 * c][...] = _dot_nt(u_ref[c * 2 * LANE:(c + 1) * 2 * LANE, :], xn_s[...])

    def accumulate(c):
        acc_s[...] += _dot(vt_ref[:, c * 2 * LANE:(c + 1) * 2 * LANE], slabs[2 * c + 1][...])

    hidden(0)
    for c in range(nb // 2):
        ht_s, a_s = slabs[2 * c], slabs[2 * c + 1]
        if c + 1 < nb // 2:
            hidden(c + 1)
        for i in range(2):
            r = slice(i * LANE, (i + 1) * LANE)
            i1 = j * nb + 2 * c + i
            for g in range(ng):
                cols = slice(g * LANE, (g + 1) * LANE)
                w = jnp.zeros((LANE, LANE), F32)
                for h in range(PEER_HEADS):
                    z = e1_s[g, h, pl.ds(i1, 1), :] * e2_s[g, h]
                    w = w + jnp.where(z >= th_s[g, h, 0:1, :], z, 0.0)
                a_s[r, cols] = (w * _gelu(ht_s[r, cols])).astype(BF16)
        if c >= 1:
            accumulate(c - 1)
    accumulate(nb // 2 - 1)

    @pl.when(j == pl.num_programs(1) - 1)
    def _finish():
        x2 = x1_ref[...] + acc_s[...].T
        y_out[...] = _rms(x2) * nfin_ref[...]


def _peer(x1, w, tm, te):
    T, D = x1.shape
    NE = w["u"].shape[0]
    assert T % tm == 0 and NE % te == 0 and tm % LANE == 0 and te % LANE == 0
    ng = tm // LANE
    consts = [w["nffn"], w["wqt"], w["keys"]]
    return pl.pallas_call(
        functools.partial(_peer_kernel, te=te),
        grid=(T // tm, NE // te),
        in_specs=[pl.BlockSpec((tm, D), lambda i, j: (i, 0))]
        + [_const_spec(c.shape) for c in consts]
        + [pl.BlockSpec((te, D), lambda i, j: (j, 0)),
           pl.BlockSpec((D, te), lambda i, j: (0, j)),
           _const_spec(w["nfin"].shape)],
        out_specs=pl.BlockSpec((tm, D), lambda i, j: (i, 0)),
        out_shape=jax.ShapeDtypeStruct((T, D), F32),
        scratch_shapes=[
            pltpu.VMEM((tm, D), BF16),
            pltpu.VMEM((PEER_HEADS * 2 * PEER_HALF, tm), BF16),
            pltpu.VMEM((ng, PEER_HEADS, N_KEYS, LANE), F32),
            pltpu.VMEM((ng, PEER_HEADS, N_KEYS, LANE), F32),
            pltpu.VMEM((ng, PEER_HEADS, 8, LANE), F32),
            pltpu.VMEM((D, tm), F32),
        ] + [pltpu.VMEM((2 * LANE, tm), F32),
             pltpu.VMEM((2 * LANE, tm), BF16)] * (te // (2 * LANE)),
        compiler_params=pltpu.CompilerParams(
            dimension_semantics=("parallel", "arbitrary"), vmem_limit_bytes=VMEM_LIMIT),
        name="peer",
    )(x1, *consts, w["u"], w["vt"], w["nfin"])


def _rope_tables(seq):
    inv = 1.0 / (ROPE_THETA ** (jnp.arange(0, QK_ROPE, 2, dtype=F32) / QK_ROPE))
    ang = jnp.arange(seq, dtype=F32)[:, None] * inv[None, :]
    cos, sin = jnp.cos(ang), jnp.sin(ang)
    pad = jnp.zeros((seq, LANE - QK_ROPE), F32)
    return (jnp.concatenate([cos, cos, pad], axis=1), jnp.concatenate([sin, sin, pad], axis=1))


def _rot_cols(wpe):
    half = QK_ROPE // 2
    return jnp.concatenate([-wpe[..., half:], wpe[..., :half]], axis=-1)


def _prep_weights(norm_mix, w_in, q_norm, kv_norm, w_uq, w_ukv, w_o_b, sgu_norm, sgu_w, sgu_b,
                  w_o_a, w_out, norm_ffn, peer_wq, peer_keys, peer_u, peer_v, norm_final):
    a, b, c = Q_LORA, Q_LORA + KV_LORA, Q_LORA + KV_LORA + QK_ROPE
    d, e, f = c + SGU_WIDTH, c + 2 * SGU_WIDTH, c + 2 * SGU_WIDTH + D_MODEL
    zpad = lambda rows, n: jnp.zeros((rows, n), F32)
    w_kr = w_in[:, b:c]
    wkr = jnp.concatenate([w_kr, zpad(D_MODEL, LANE - QK_ROPE),
                           _rot_cols(w_kr), zpad(D_MODEL, LANE - QK_ROPE)], axis=1)
    uq = w_uq.reshape(Q_LORA, MLA_HEADS, QK_NOPE + QK_ROPE)
    hpad = jnp.zeros((Q_LORA, MLA_HEADS, LANE - QK_ROPE), F32)
    wuq = jnp.concatenate([uq, hpad], axis=-1).reshape(Q_LORA, MLA_HEADS * QK_PAD)
    wuqr = jnp.concatenate([_rot_cols(uq[..., QK_NOPE:]), hpad], axis=-1).reshape(Q_LORA, MLA_HEADS * LANE)
    ukv = w_ukv.reshape(KV_LORA, MLA_HEADS, QK_NOPE + V_HEAD)
    scale = (QK_NOPE + QK_ROPE) ** -0.5 * LOG2E
    sbias = jnp.repeat(sgu_b.T, SGU_WIDTH // SGU_GROUPS, axis=1)
    row = lambda v: v.reshape(1, -1).astype(F32)
    return {
        "nmix": row(norm_mix), "wq": w_in[:, :a].astype(BF16), "wkv": w_in[:, a:b].astype(BF16),
        "wkr": wkr.astype(BF16), "wu": w_in[:, c:d].astype(BF16), "wv": w_in[:, d:e].astype(BF16),
        "wga": w_in[:, e:f].astype(BF16), "wgb": w_in[:, f:].astype(BF16),
        "qn": row(q_norm) * scale, "kvn": row(kv_norm),
        "wuq": wuq.astype(BF16), "wuqr": wuqr.astype(BF16),
        "wuk": ukv[..., :QK_NOPE].reshape(KV_LORA, MLA_HEADS * QK_NOPE).astype(BF16),
        "wuvt": ukv[..., QK_NOPE:].reshape(KV_LORA, MLA_HEADS * V_HEAD).T.astype(BF16),
        "sgn": row(sgu_norm), "sw": sgu_w.astype(BF16), "sbias": sbias.astype(F32),
        "woa": w_o_a.astype(BF16), "wob": w_o_b.astype(BF16), "wout": w_out.astype(BF16),
        "nffn": row(norm_ffn), "wqt": peer_wq.T.astype(BF16), "keys": peer_keys.astype(BF16),
        "u": peer_u.astype(BF16), "vt": peer_v.T.astype(BF16), "nfin": row(norm_final),
    }


def _trunk(x, w, front_tm=FRONT_TM, tq=ATTN_TQ, tk=ATTN_TK, post_tm=POST_TM,
           peer_tm=PEER_TM, peer_te=PEER_TE):
    B, S, D = x.shape
    cos, sin = _rope_tables(S)
    q, k, vt, ya, sgb = _front(x, cos, sin, w, front_tm)
    o = _attention(q, k, vt, tq, tk)
    flat = lambda t: t.reshape(B * S, t.shape[-1])
    x1 = _post(flat(x), flat(o), flat(ya), flat(sgb), w["wob"], w["wout"], post_tm)
    y = _peer(x1, w, peer_tm, peer_te)
    return y.reshape(B, S, D)


def kernel(x_prompt, x_sample, norm_mix, w_in, q_norm, kv_norm, w_uq, w_ukv, w_o_b, sgu_norm,
           sgu_w, sgu_b, w_o_a, w_out, norm_ffn, peer_wq, peer_keys, peer_u, peer_v, norm_final):
    assert norm_mix.shape[0] == 1, "single layer"
    w = _prep_weights(norm_mix[0], w_in[0], q_norm[0], kv_norm[0], w_uq[0], w_ukv[0], w_o_b[0],
                      sgu_norm[0], sgu_w[0], sgu_b[0], w_o_a[0], w_out[0], norm_ffn[0],
                      peer_wq[0], peer_keys[0], peer_u[0], peer_v[0], norm_final)
    return (_trunk(x_prompt, w), _trunk(x_sample, w))
```

```python
import functools

import jax
import jax.numpy as jnp
from jax import lax
from jax.experimental import pallas as pl
from jax.experimental.pallas import tpu as pltpu

F32 = jnp.float32
BF16 = jnp.bfloat16

D_MODEL = 1024
SGU_CHUNK = 128
SGU_GROUPS = 8
SGU_WIDTH = 1024
MLA_HEADS = 8
QK_NOPE = 128
QK_ROPE = 64
V_HEAD = 128
Q_LORA = 384
KV_LORA = 256
ROPE_THETA = 10000.0
PEER_HEADS = 8
N_KEYS = 128
PEER_HALF = 128
PEER_TOPK = 16
EPS = 1e-6

LANE = 128
BF16_ROWS = 16
QK_PAD = 256
LOG2E = 1.4426950408889634
VMEM_LIMIT = 56 * 1024 * 1024

FRONT_TM = 512
ATTN_TQ = 1024
ATTN_TK = 1024
ATTN_UNROLL = 4
POST_TM = 1024
PEER_TM = 512
PEER_TE = 2048
ROUTE_UNROLL = 4


def _rms(x, eps=EPS):
    return x * lax.rsqrt(jnp.mean(x * x, axis=-1, keepdims=True) + eps)


_GELU_K0 = -2.0 * 0.7978845608028654 * LOG2E
_GELU_K1 = _GELU_K0 * 0.044715


def _gelu(x):
    return x / (1.0 + jnp.exp2(x * (_GELU_K0 + _GELU_K1 * (x * x))))


def _dot(a, b):
    return jnp.dot(a, b, preferred_element_type=F32)


def _dot_nt(a, b):
    return lax.dot_general(a, b, (((1,), (1,)), ((), ())), preferred_element_type=F32)


def _const_spec(shape):
    nd = len(shape)
    return pl.BlockSpec(shape, lambda *_: (0,) * nd, pipeline_mode=pl.Buffered(1))


def _front_kernel(x_ref, cos_ref, sin_ref, nmix_ref, wq_ref, wkv_ref, wkr_ref, wu_ref, wv_ref,
                  wga_ref, wgb_ref, qn_ref, kvn_ref, wuq_ref, wuqr_ref, wuk_ref, wuvt_ref,
                  sgn_ref, sw_ref, sbias_ref, woa_ref,
                  q_out, k_out, vt_out, ya_out, sgb_out, vn_s, sgu_s):
    tm = x_ref.shape[1]
    x = x_ref[0]
    xn = (_rms(x) * nmix_ref[...]).astype(BF16)
    cos = cos_ref[...]
    sin = sin_ref[...]

    c_q = (_rms(_dot(xn, wq_ref[...])) * qn_ref[...]).astype(BF16)
    qa = _dot(c_q, wuq_ref[...])
    qr = _dot(c_q, wuqr_ref[...])
    for h in range(MLA_HEADS):
        q_out[0, :, h * QK_PAD:h * QK_PAD + LANE] = qa[:, h * QK_PAD:h * QK_PAD + LANE].astype(BF16)
        pe = qa[:, h * QK_PAD + LANE:(h + 1) * QK_PAD] * cos + qr[:, h * LANE:(h + 1) * LANE] * sin
        q_out[0, :, h * QK_PAD + LANE:(h + 1) * QK_PAD] = pe.astype(BF16)

    c_kv = (_rms(_dot(xn, wkv_ref[...])) * kvn_ref[...]).astype(BF16)
    kn = _dot(c_kv, wuk_ref[...])
    kr = _dot(xn, wkr_ref[...])
    kpe = (kr[:, :LANE] * cos + kr[:, LANE:] * sin).astype(BF16)
    for h in range(MLA_HEADS):
        k_out[0, :, h * QK_PAD:h * QK_PAD + LANE] = kn[:, h * LANE:(h + 1) * LANE].astype(BF16)
        k_out[0, :, h * QK_PAD + LANE:(h + 1) * QK_PAD] = kpe
    vt_out[0] = _dot_nt(wuvt_ref[...], c_kv).astype(BF16)

    v = _gelu(_dot(xn, wv_ref[...]))
    vn_s[...] = (_rms(v) * sgn_ref[...]).astype(BF16)
    u = _gelu(_dot(xn, wu_ref[...]))
    for c in range(tm // SGU_CHUNK):
        rows = slice(c * SGU_CHUNK, (c + 1) * SGU_CHUNK)
        for g in range(SGU_GROUPS):
            cols = slice(g * LANE, (g + 1) * LANE)
            mixed = _dot(sw_ref[g], vn_s[rows, cols]) + sbias_ref[:, cols]
            sgu_s[rows, cols] = (u[rows, cols] * mixed).astype(BF16)
    ya = _dot(sgu_s[...], woa_ref[...])
    ya_out[0] = (jax.nn.sigmoid(_dot(xn, wga_ref[...])) * ya).astype(BF16)
    sgb_out[0] = jax.nn.sigmoid(_dot(xn, wgb_ref[...])).astype(BF16)


def _front(x, cos, sin, w, tm):
    B, S, D = x.shape
    assert S % tm == 0 and tm % SGU_CHUNK == 0
    nt = S // tm
    tok = lambda width: pl.BlockSpec((1, tm, width), lambda b, i: (b, i, 0))
    pos = pl.BlockSpec((tm, LANE), lambda b, i: (i, 0))
    consts = [w["nmix"], w["wq"], w["wkv"], w["wkr"], w["wu"], w["wv"], w["wga"], w["wgb"],
              w["qn"], w["kvn"], w["wuq"], w["wuqr"], w["wuk"], w["wuvt"], w["sgn"], w["sw"],
              w["sbias"], w["woa"]]
    out_shape = (
        jax.ShapeDtypeStruct((B, S, MLA_HEADS * QK_PAD), BF16),
        jax.ShapeDtypeStruct((B, S, MLA_HEADS * QK_PAD), BF16),
        jax.ShapeDtypeStruct((B, MLA_HEADS * V_HEAD, S), BF16),
        jax.ShapeDtypeStruct((B, S, D), BF16),
        jax.ShapeDtypeStruct((B, S, D), BF16),
    )
    return pl.pallas_call(
        _front_kernel,
        grid=(B, nt),
        in_specs=[tok(D), pos, pos] + [_const_spec(c.shape) for c in consts],
        out_specs=(tok(MLA_HEADS * QK_PAD), tok(MLA_HEADS * QK_PAD),
                   pl.BlockSpec((1, MLA_HEADS * V_HEAD, tm), lambda b, i: (b, 0, i)),
                   tok(D), tok(D)),
        out_shape=out_shape,
        scratch_shapes=[pltpu.VMEM((tm, SGU_WIDTH), BF16), pltpu.VMEM((tm, SGU_WIDTH), BF16)],
        compiler_params=pltpu.CompilerParams(
            dimension_semantics=("parallel", "parallel"), vmem_limit_bytes=VMEM_LIMIT),
        name="front",
    )(x, cos, sin, *consts)


def _attn_kernel(q_ref, k_ref, vt_ref, o_ref, sa_s, sb_s, *, tk):
    tq = q_ref.shape[1]
    nk = k_ref.shape[1] // tk
    q = q_ref[0]

    def scores(s_ref, c):
        start = pl.multiple_of(c * tk, tk)
        s = _dot_nt(k_ref[0, pl.ds(start, tk), :], q)
        s_ref[...] = s
        return jnp.max(s, axis=0, keepdims=True)

    def update(s_ref, mc, c, carry):
        m, l, acc = carry
        start = pl.multiple_of(c * tk, tk)
        m_new = jnp.maximum(m, mc)
        alpha = jnp.exp2(m - m_new)
        p = jnp.exp2(s_ref[...] - m_new)
        l = alpha * l + jnp.sum(p, axis=0, keepdims=True)
        acc = alpha * acc + _dot(vt_ref[0, :, pl.ds(start, tk)], p.astype(BF16))
        return m_new, l, acc

    def run(c0, n, mc, state, more):
        bufs = (sa_s, sb_s)
        for t in range(n):
            mc_next = scores(bufs[(t + 1) % 2], c0 + t + 1) if (t + 1 < n or more) else None
            state = update(bufs[t % 2], mc, c0 + t, state)
            mc = mc_next
        return mc, state

    unroll = ATTN_UNROLL if nk >= 4 * ATTN_UNROLL else 2

    def body(i, carry):
        return run(i * unroll, unroll, *carry, more=True)

    state = (jnp.full((1, tq), -jnp.inf, F32), jnp.zeros((1, tq), F32), jnp.zeros((V_HEAD, tq), F32))
    nloop = nk // unroll - 1
    mc, state = lax.fori_loop(0, nloop, body, (scores(sa_s, 0), state))
    _, (_, l, acc) = run(nloop * unroll, unroll, mc, state, more=False)
    o_ref[0] = (acc / l).T.astype(o_ref.dtype)


def _attention(q, k, vt, tq, tk):
    B, S, _ = q.shape
    assert S % tq == 0 and S % (ATTN_UNROLL * tk) == 0 and ATTN_UNROLL % 4 == 0
    return pl.pallas_call(
        functools.partial(_attn_kernel, tk=tk),
        scratch_shapes=[pltpu.VMEM((tk, tq), F32)] * 2,
        grid=(B, MLA_HEADS, S // tq),
        in_specs=[
            pl.BlockSpec((1, tq, QK_PAD), lambda b, h, i: (b, i, h)),
            pl.BlockSpec((1, S, QK_PAD), lambda b, h, i: (b, 0, h)),
            pl.BlockSpec((1, V_HEAD, S), lambda b, h, i: (b, h, 0)),
        ],
        out_specs=pl.BlockSpec((1, tq, V_HEAD), lambda b, h, i: (b, i, h)),
        out_shape=jax.ShapeDtypeStruct((B, S, MLA_HEADS * V_HEAD), BF16),
        compiler_params=pltpu.CompilerParams(
            dimension_semantics=("parallel", "parallel", "arbitrary"),
            vmem_limit_bytes=VMEM_LIMIT),
        name="attn",
    )(q, k, vt)


def _post_kernel(x_ref, o_ref, ya_ref, sgb_ref, wob_ref, wout_ref, x1_out):
    yb = _dot(o_ref[...], wob_ref[...])
    merged = ya_ref[...].astype(F32) + sgb_ref[...].astype(F32) * yb
    x1_out[...] = x_ref[...] + _dot(merged.astype(BF16), wout_ref[...])


def _post(x2, o2, ya2, sgb2, wob, wout, tm):
    T, D = x2.shape
    assert T % tm == 0
    tok = pl.BlockSpec((tm, D), lambda i: (i, 0))
    return pl.pallas_call(
        _post_kernel,
        grid=(T // tm,),
        in_specs=[tok, tok, tok, tok, _const_spec(wob.shape), _const_spec(wout.shape)],
        out_specs=tok,
        out_shape=jax.ShapeDtypeStruct((T, D), F32),
        compiler_params=pltpu.CompilerParams(
            dimension_semantics=("parallel",), vmem_limit_bytes=VMEM_LIMIT),
        name="post",
    )(x2, o2, ya2, sgb2, wob, wout)


def _col_max(a):
    return jnp.max(a, axis=0, keepdims=True)


def _sorting_network(n):
    comps = []
    p = 1
    while p < n:
        k = p
        while k >= 1:
            for j in range(k % p, n - k, 2 * k):
                for i in range(min(k, n - j - k)):
                    if (i + j) // (2 * p) == (i + j + k) // (2 * p):
                        comps.append((i + j, i + j + k))
            k //= 2
        p *= 2
    return comps


def _merge_heads(lists, extra, n):
    lists = list(lists)
    vals = []
    for t in range(n):
        head = lists[0] if extra is None else jnp.maximum(lists[0], extra)
        m = jnp.maximum(_col_max(head), 0.0)
        vals.append(m)
        if t == n - 1:
            break
        hit = lists[0] == m
        live = min(len(lists) - 1, n - 1 - t)
        for d in range(live):
            lists[d] = jnp.where(hit, lists[d + 1], lists[d])
        if live == len(lists) - 1:
            lists[live] = jnp.where(hit, -1.0, lists[live])
        if extra is not None:
            extra = jnp.where(extra == m, -1.0, extra)
    return vals


def _top_values(e, n):
    groups = [e[8 * r:8 * r + 8] for r in range(e.shape[0] // 8)]
    for a, b in _sorting_network(len(groups)):
        groups[a], groups[b] = jnp.maximum(groups[a], groups[b]), jnp.minimum(groups[a], groups[b])
    return jnp.concatenate(_merge_heads(groups[:n], None, n), axis=0)


def _peer_kernel(x1_ref, nffn_ref, wqt_ref, keys_ref, u_ref, vt_ref, nfin_ref, y_out,
                 xn_s, qt_s, e1_s, e2_s, th_s, acc_s, *slabs, te):
    tm = x1_ref.shape[0]
    ng = tm // LANE
    nb = te // LANE
    j = pl.program_id(1)

    @pl.when(j == 0)
    def _route():
        xn = (_rms(x1_ref[...]) * nffn_ref[...]).astype(BF16)
        xn_s[...] = xn
        qt_s[...] = _dot_nt(wqt_ref[...], xn).astype(BF16)
        acc_s[...] = jnp.zeros_like(acc_s)
        for h in range(PEER_HEADS):
            for p, e_s in ((0, e1_s), (1, e2_s)):
                r0 = (2 * h + p) * PEER_HALF
                s = _dot(keys_ref[p], qt_s[r0:r0 + PEER_HALF, :])
                e = jnp.exp(s - _col_max(s))
                for g in range(ng):
                    e_s[g, h] = e[:, g * LANE:(g + 1) * LANE]

        def route_one(g, h):
            e1 = e1_s[g, h]
            e2 = e2_s[g, h]
            v1 = _top_values(e1, PEER_TOPK)
            v2 = _top_values(e2, PEER_TOPK)
            half = PEER_TOPK // 2

            def products(w2):
                return ([v1[:half] * w2[b:b + 1] for b in range(PEER_TOPK)], v1[half:] * w2[0:1])

            cand = products(v2)
            th = _merge_heads(*cand, PEER_TOPK)[-1]
            sel = ([c >= th for c in cand[0]], cand[1] >= th)

            def selected(prod, fill, combine):
                out = jnp.where(sel[1], prod[1], fill)
                for s, c in zip(sel[0], prod[0]):
                    out = combine(out, jnp.where(s, c, fill))
                return out

            inv_z = 1.0 / jnp.sum(selected(cand, 0.0, jnp.add), axis=0, keepdims=True)
            e2_s[g, h] = e2 * inv_z
            thn = jnp.min(selected(products(v2 * inv_z), jnp.inf, jnp.minimum), axis=0, keepdims=True)
            th_s[g, h] = jnp.broadcast_to(thn, (8, LANE))

        def route_group(i, _):
            for k in range(ROUTE_UNROLL):
                item = ROUTE_UNROLL * i + k
                route_one(item // PEER_HEADS, item % PEER_HEADS)
            return 0

        lax.fori_loop(0, ng * PEER_HEADS // ROUTE_UNROLL, route_group, 0)

    def hidden(c):
        slabs[2 * c][...] = _dot_nt(u_ref[c * 2 * LANE:(c + 1) * 2 * LANE, :], xn_s[...])

    def accumulate(c):
        acc_s[...] += _dot(vt_ref[0, :, c * 2 * LANE:(c + 1) * 2 * LANE], slabs[2 * c + 1][...])

    hidden(0)
    for c in range(nb // 2):
        ht_s, a_s = slabs[2 * c], slabs[2 * c + 1]
        if c + 1 < nb // 2:
            hidden(c + 1)
        for i in range(2):
            r = slice(i * LANE, (i + 1) * LANE)
            i1 = j * nb + 2 * c + i
            for g in range(ng):
                cols = slice(g * LANE, (g + 1) * LANE)
                w = jnp.zeros((LANE, LANE), F32)
                for h in range(PEER_HEADS):
                    z = e1_s[g, h, pl.ds(i1, 1), :] * e2_s[g, h]
                    w = w + jnp.where(z >= th_s[g, h, 0:1, :], z, 0.0)
                a_s[r, cols] = (w * _gelu(ht_s[r, cols])).astype(BF16)
        if c >= 1:
            accumulate(c - 1)
    accumulate(nb // 2 - 1)

    @pl.when(j == pl.num_programs(1) - 1)
    def _finish():
        x2 = x1_ref[...] + acc_s[...].T
        y_out[...] = _rms(x2) * nfin_ref[...]


def _peer(x1, w, tm):
    T, D = x1.shape
    NE = w["u"].shape[0]
    te = w["vt"].shape[2]
    assert T % tm == 0 and NE % te == 0 and tm % LANE == 0 and te % (2 * LANE) == 0
    ng = tm // LANE
    consts = [w["nffn"], w["wqt"], w["keys"]]
    return pl.pallas_call(
        functools.partial(_peer_kernel, te=te),
        grid=(T // tm, NE // te),
        in_specs=[pl.BlockSpec((tm, D), lambda i, j: (i, 0))]
        + [_const_spec(c.shape) for c in consts]
        + [pl.BlockSpec((te, D), lambda i, j: (j, 0)),
           pl.BlockSpec((1, D, te), lambda i, j: (j, 0, 0)),
           _const_spec(w["nfin"].shape)],
        out_specs=pl.BlockSpec((tm, D), lambda i, j: (i, 0)),
        out_shape=jax.ShapeDtypeStruct((T, D), F32),
        scratch_shapes=[
            pltpu.VMEM((tm, D), BF16),
            pltpu.VMEM((PEER_HEADS * 2 * PEER_HALF, tm), BF16),
            pltpu.VMEM((ng, PEER_HEADS, N_KEYS, LANE), F32),
            pltpu.VMEM((ng, PEER_HEADS, N_KEYS, LANE), F32),
            pltpu.VMEM((ng, PEER_HEADS, 8, LANE), F32),
            pltpu.VMEM((D, tm), F32),
        ] + [pltpu.VMEM((2 * LANE, tm), F32),
             pltpu.VMEM((2 * LANE, tm), BF16)] * (te // (2 * LANE)),
        compiler_params=pltpu.CompilerParams(
            dimension_semantics=("parallel", "arbitrary"), vmem_limit_bytes=VMEM_LIMIT),
        name="peer",
    )(x1, *consts, w["u"], w["vt"], w["nfin"])


def _rope_tables(seq):
    inv = 1.0 / (ROPE_THETA ** (jnp.arange(0, QK_ROPE, 2, dtype=F32) / QK_ROPE))
    ang = jnp.arange(seq, dtype=F32)[:, None] * inv[None, :]
    cos, sin = jnp.cos(ang), jnp.sin(ang)
    pad = jnp.zeros((seq, LANE - QK_ROPE), F32)
    return (jnp.concatenate([cos, cos, pad], axis=1), jnp.concatenate([sin, sin, pad], axis=1))


def _rot_cols(wpe):
    half = QK_ROPE // 2
    return jnp.concatenate([-wpe[..., half:], wpe[..., :half]], axis=-1)


def _prep_weights(norm_mix, w_in, q_norm, kv_norm, w_uq, w_ukv, w_o_b, sgu_norm, sgu_w, sgu_b,
                  w_o_a, w_out, norm_ffn, peer_wq, peer_keys, peer_u, peer_v, norm_final):
    a, b, c = Q_LORA, Q_LORA + KV_LORA, Q_LORA + KV_LORA + QK_ROPE
    d, e, f = c + SGU_WIDTH, c + 2 * SGU_WIDTH, c + 2 * SGU_WIDTH + D_MODEL
    zpad = lambda rows, n: jnp.zeros((rows, n), F32)
    w_kr = w_in[:, b:c]
    wkr = jnp.concatenate([w_kr, zpad(D_MODEL, LANE - QK_ROPE),
                           _rot_cols(w_kr), zpad(D_MODEL, LANE - QK_ROPE)], axis=1)
    uq = w_uq.reshape(Q_LORA, MLA_HEADS, QK_NOPE + QK_ROPE)
    hpad = jnp.zeros((Q_LORA, MLA_HEADS, LANE - QK_ROPE), F32)
    wuq = jnp.concatenate([uq, hpad], axis=-1).reshape(Q_LORA, MLA_HEADS * QK_PAD)
    wuqr = jnp.concatenate([_rot_cols(uq[..., QK_NOPE:]), hpad], axis=-1).reshape(Q_LORA, MLA_HEADS * LANE)
    ukv = w_ukv.reshape(KV_LORA, MLA_HEADS, QK_NOPE + V_HEAD)
    scale = (QK_NOPE + QK_ROPE) ** -0.5 * LOG2E
    sbias = jnp.repeat(sgu_b.T, SGU_WIDTH // SGU_GROUPS, axis=1)
    row = lambda v: v.reshape(1, -1).astype(F32)
    return {
        "nmix": row(norm_mix), "wq": w_in[:, :a].astype(BF16), "wkv": w_in[:, a:b].astype(BF16),
        "wkr": wkr.astype(BF16), "wu": w_in[:, c:d].astype(BF16), "wv": w_in[:, d:e].astype(BF16),
        "wga": w_in[:, e:f].astype(BF16), "wgb": w_in[:, f:].astype(BF16),
        "qn": row(q_norm) * scale, "kvn": row(kv_norm),
        "wuq": wuq.astype(BF16), "wuqr": wuqr.astype(BF16),
        "wuk": ukv[..., :QK_NOPE].reshape(KV_LORA, MLA_HEADS * QK_NOPE).astype(BF16),
        "wuvt": ukv[..., QK_NOPE:].reshape(KV_LORA, MLA_HEADS * V_HEAD).T.astype(BF16),
        "sgn": row(sgu_norm), "sw": sgu_w.astype(BF16), "sbias": sbias.astype(F32),
        "woa": w_o_a.astype(BF16), "wob": w_o_b.astype(BF16), "wout": w_out.astype(BF16),
        "nffn": row(norm_ffn), "wqt": peer_wq.T.astype(BF16), "keys": peer_keys.astype(BF16),
        "u": peer_u.astype(BF16), "nfin": row(norm_final),
        "vt": peer_v.reshape(-1, PEER_TE, D_MODEL).transpose(0, 2, 1).astype(BF16),
    }


def _trunk(x, w, front_tm=FRONT_TM, tq=ATTN_TQ, tk=ATTN_TK, post_tm=POST_TM,
           peer_tm=PEER_TM):
    B, S, D = x.shape
    cos, sin = _rope_tables(S)
    q, k, vt, ya, sgb = _front(x, cos, sin, w, front_tm)
    o = _attention(q, k, vt, tq, tk)
    flat = lambda t: t.reshape(B * S, t.shape[-1])
    x1 = _post(flat(x), flat(o), flat(ya), flat(sgb), w["wob"], w["wout"], post_tm)
    y = _peer(x1, w, peer_tm)
    return y.reshape(B, S, D)


def kernel(x_prompt, x_sample, norm_mix, w_in, q_norm, kv_norm, w_uq, w_ukv, w_o_b, sgu_norm,
           sgu_w, sgu_b, w_o_a, w_out, norm_ffn, peer_wq, peer_keys, peer_u, peer_v, norm_final):
    assert norm_mix.shape[0] == 1, "single layer"
    w = _prep_weights(norm_mix[0], w_in[0], q_norm[0], kv_norm[0], w_uq[0], w_ukv[0], w_o_b[0],
                      sgu_norm[0], sgu_w[0], sgu_b[0], w_o_a[0], w_out[0], norm_ffn[0],
                      peer_wq[0], peer_keys[0], peer_u[0], peer_v[0], norm_final)
    return (_trunk(x_prompt, w), _trunk(x_sample, w))
```

```python
import functools

import jax
import jax.numpy as jnp
from jax import lax
from jax.experimental import pallas as pl
from jax.experimental.pallas import tpu as pltpu

F32 = jnp.float32
BF16 = jnp.bfloat16

D_MODEL = 1024
SGU_CHUNK = 128
SGU_GROUPS = 8
SGU_WIDTH = 1024
MLA_HEADS = 8
QK_NOPE = 128
QK_ROPE = 64
V_HEAD = 128
Q_LORA = 384
KV_LORA = 256
ROPE_THETA = 10000.0
PEER_HEADS = 8
N_KEYS = 128
PEER_HALF = 128
PEER_TOPK = 16
EPS = 1e-6

LANE = 128
SUBLANE = 8
QK_PAD = 256
LOG2E = 1.4426950408889634
VMEM_LIMIT = 56 * 1024 * 1024

FRONT_TM = 512
ATTN_TQ = 1024
ATTN_TK = 1024
ATTN_UNROLL = 4
POST_TM = 1024
PEER_TM = 512
PEER_TE = 2048
ROUTE_UNROLL = 8


def _rms(x, eps=EPS):
    return x * lax.rsqrt(jnp.mean(x * x, axis=-1, keepdims=True) + eps)


_GELU_K0 = -2.0 * 0.7978845608028654 * LOG2E
_GELU_K1 = _GELU_K0 * 0.044715


def _gelu(x):
    return x / (1.0 + jnp.exp2(x * (_GELU_K0 + _GELU_K1 * (x * x))))


def _dot(a, b):
    return jnp.dot(a, b, preferred_element_type=F32)


def _dot_nt(a, b):
    return lax.dot_general(a, b, (((1,), (1,)), ((), ())), preferred_element_type=F32)


def _const_spec(shape):
    nd = len(shape)
    return pl.BlockSpec(shape, lambda *_: (0,) * nd, pipeline_mode=pl.Buffered(1))


def _front_kernel(x_ref, cos_ref, sin_ref, nmix_ref, wq_ref, wkv_ref, wkr_ref, wu_ref, wv_ref,
                  wga_ref, wgb_ref, qn_ref, kvn_ref, wuq_ref, wuqr_ref, wuk_ref, wuvt_ref,
                  sgn_ref, sw_ref, sbias_ref, woa_ref,
                  q_out, k_out, vt_out, ya_out, sgb_out, vn_s, sgu_s):
    tm = x_ref.shape[1]
    x = x_ref[0]
    xn = (_rms(x) * nmix_ref[...]).astype(BF16)
    cos = cos_ref[...]
    sin = sin_ref[...]

    c_q = (_rms(_dot(xn, wq_ref[...])) * qn_ref[...]).astype(BF16)
    qa = _dot(c_q, wuq_ref[...])
    qr = _dot(c_q, wuqr_ref[...])
    for h in range(MLA_HEADS):
        q_out[0, :, h * QK_PAD:h * QK_PAD + LANE] = qa[:, h * QK_PAD:h * QK_PAD + LANE].astype(BF16)
        pe = qa[:, h * QK_PAD + LANE:(h + 1) * QK_PAD] * cos + qr[:, h * LANE:(h + 1) * LANE] * sin
        q_out[0, :, h * QK_PAD + LANE:(h + 1) * QK_PAD] = pe.astype(BF16)

    c_kv = (_rms(_dot(xn, wkv_ref[...])) * kvn_ref[...]).astype(BF16)
    kn = _dot(c_kv, wuk_ref[...])
    kr = _dot(xn, wkr_ref[...])
    kpe = (kr[:, :LANE] * cos + kr[:, LANE:] * sin).astype(BF16)
    for h in range(MLA_HEADS):
        k_out[0, :, h * QK_PAD:h * QK_PAD + LANE] = kn[:, h * LANE:(h + 1) * LANE].astype(BF16)
        k_out[0, :, h * QK_PAD + LANE:(h + 1) * QK_PAD] = kpe
    vt_out[0] = _dot_nt(wuvt_ref[...], c_kv).astype(BF16)

    v = _gelu(_dot(xn, wv_ref[...]))
    vn_s[...] = (_rms(v) * sgn_ref[...]).astype(BF16)
    u = _gelu(_dot(xn, wu_ref[...]))
    for c in range(tm // SGU_CHUNK):
        rows = slice(c * SGU_CHUNK, (c + 1) * SGU_CHUNK)
        for g in range(SGU_GROUPS):
            cols = slice(g * LANE, (g + 1) * LANE)
            mixed = _dot(sw_ref[g], vn_s[rows, cols]) + sbias_ref[:, cols]
            sgu_s[rows, cols] = (u[rows, cols] * mixed).astype(BF16)
    ya = _dot(sgu_s[...], woa_ref[...])
    ya_out[0] = (jax.nn.sigmoid(_dot(xn, wga_ref[...])) * ya).astype(BF16)
    sgb_out[0] = jax.nn.sigmoid(_dot(xn, wgb_ref[...])).astype(BF16)


def _front(x, cos, sin, w, tm):
    B, S, D = x.shape
    assert S % tm == 0 and tm % SGU_CHUNK == 0
    nt = S // tm
    tok = lambda width: pl.BlockSpec((1, tm, width), lambda b, i: (b, i, 0))
    pos = pl.BlockSpec((tm, LANE), lambda b, i: (i, 0))
    consts = [w["nmix"], w["wq"], w["wkv"], w["wkr"], w["wu"], w["wv"], w["wga"], w["wgb"],
              w["qn"], w["kvn"], w["wuq"], w["wuqr"], w["wuk"], w["wuvt"], w["sgn"], w["sw"],
              w["sbias"], w["woa"]]
    out_shape = (
        jax.ShapeDtypeStruct((B, S, MLA_HEADS * QK_PAD), BF16),
        jax.ShapeDtypeStruct((B, S, MLA_HEADS * QK_PAD), BF16),
        jax.ShapeDtypeStruct((B, MLA_HEADS * V_HEAD, S), BF16),
        jax.ShapeDtypeStruct((B, S, D), BF16),
        jax.ShapeDtypeStruct((B, S, D), BF16),
    )
    return pl.pallas_call(
        _front_kernel,
        grid=(B, nt),
        in_specs=[tok(D), pos, pos] + [_const_spec(c.shape) for c in consts],
        out_specs=(tok(MLA_HEADS * QK_PAD), tok(MLA_HEADS * QK_PAD),
                   pl.BlockSpec((1, MLA_HEADS * V_HEAD, tm), lambda b, i: (b, 0, i)),
                   tok(D), tok(D)),
        out_shape=out_shape,
        scratch_shapes=[pltpu.VMEM((tm, SGU_WIDTH), BF16), pltpu.VMEM((tm, SGU_WIDTH), BF16)],
        compiler_params=pltpu.CompilerParams(
            dimension_semantics=("parallel", "parallel"), vmem_limit_bytes=VMEM_LIMIT),
        name="front",
    )(x, cos, sin, *consts)


def _attn_kernel(q_ref, k_ref, vt_ref, o_ref, sa_s, sb_s, *, tk):
    tq = q_ref.shape[1]
    nk = k_ref.shape[1] // tk
    q = q_ref[0]

    def scores(s_ref, c):
        start = pl.multiple_of(c * tk, tk)
        s = _dot_nt(k_ref[0, pl.ds(start, tk), :], q)
        s_ref[...] = s
        return jnp.max(s, axis=0, keepdims=True)

    def update(s_ref, mc, c, carry):
        m, l, acc = carry
        start = pl.multiple_of(c * tk, tk)
        m_new = jnp.maximum(m, mc)
        alpha = jnp.exp2(m - m_new)
        p = jnp.exp2(s_ref[...] - m_new)
        l = alpha * l + jnp.sum(p, axis=0, keepdims=True)
        acc = alpha * acc + _dot(vt_ref[0, :, pl.ds(start, tk)], p.astype(BF16))
        return m_new, l, acc

    def run(c0, n, mc, state, more):
        bufs = (sa_s, sb_s)
        for t in range(n):
            mc_next = scores(bufs[(t + 1) % 2], c0 + t + 1) if (t + 1 < n or more) else None
            state = update(bufs[t % 2], mc, c0 + t, state)
            mc = mc_next
        return mc, state

    unroll = ATTN_UNROLL if nk >= 4 * ATTN_UNROLL else 2

    def body(i, carry):
        return run(i * unroll, unroll, *carry, more=True)

    state = (jnp.full((1, tq), -jnp.inf, F32), jnp.zeros((1, tq), F32), jnp.zeros((V_HEAD, tq), F32))
    nloop = nk // unroll - 1
    mc, state = lax.fori_loop(0, nloop, body, (scores(sa_s, 0), state))
    _, (_, l, acc) = run(nloop * unroll, unroll, mc, state, more=False)
    o_ref[0] = (acc / l).T.astype(o_ref.dtype)


def _attention(q, k, vt, tq, tk):
    B, S, _ = q.shape
    assert S % tq == 0 and S % (ATTN_UNROLL * tk) == 0 and ATTN_UNROLL % 4 == 0
    return pl.pallas_call(
        functools.partial(_attn_kernel, tk=tk),
        scratch_shapes=[pltpu.VMEM((tk, tq), F32)] * 2,
        grid=(B, MLA_HEADS, S // tq),
        in_specs=[
            pl.BlockSpec((1, tq, QK_PAD), lambda b, h, i: (b, i, h)),
            pl.BlockSpec((1, S, QK_PAD), lambda b, h, i: (b, 0, h)),
            pl.BlockSpec((1, V_HEAD, S), lambda b, h, i: (b, h, 0)),
        ],
        out_specs=pl.BlockSpec((1, tq, V_HEAD), lambda b, h, i: (b, i, h)),
        out_shape=jax.ShapeDtypeStruct((B, S, MLA_HEADS * V_HEAD), BF16),
        compiler_params=pltpu.CompilerParams(
            dimension_semantics=("parallel", "parallel", "arbitrary"),
            vmem_limit_bytes=VMEM_LIMIT),
        name="attn",
    )(q, k, vt)


def _post_kernel(x_ref, o_ref, ya_ref, sgb_ref, wob_ref, wout_ref, x1_out):
    yb = _dot(o_ref[...], wob_ref[...])
    merged = ya_ref[...].astype(F32) + sgb_ref[...].astype(F32) * yb
    x1_out[...] = x_ref[...] + _dot(merged.astype(BF16), wout_ref[...])


def _post(x2, o2, ya2, sgb2, wob, wout, tm):
    T, D = x2.shape
    assert T % tm == 0
    tok = pl.BlockSpec((tm, D), lambda i: (i, 0))
    return pl.pallas_call(
        _post_kernel,
        grid=(T // tm,),
        in_specs=[tok, tok, tok, tok, _const_spec(wob.shape), _const_spec(wout.shape)],
        out_specs=tok,
        out_shape=jax.ShapeDtypeStruct((T, D), F32),
        compiler_params=pltpu.CompilerParams(
            dimension_semantics=("parallel",), vmem_limit_bytes=VMEM_LIMIT),
        name="post",
    )(x2, o2, ya2, sgb2, wob, wout)


def _col_max(a):
    return jnp.max(a, axis=0, keepdims=True)


def _sorting_network(n):
    comps = []
    p = 1
    while p < n:
        k = p
        while k >= 1:
            for j in range(k % p, n - k, 2 * k):
                for i in range(min(k, n - j - k)):
                    if (i + j) // (2 * p) == (i + j + k) // (2 * p):
                        comps.append((i + j, i + j + k))
            k //= 2
        p *= 2
    return comps


def _merge_heads(lists, extra, n):
    lists = list(lists)
    vals = []
    for t in range(n):
        head = lists[0] if extra is None else jnp.maximum(lists[0], extra)
        m = jnp.maximum(_col_max(head), 0.0)
        vals.append(m)
        if t == n - 1:
            break
        hit = lists[0] == m
        live = min(len(lists) - 1, n - 1 - t)
        for d in range(live):
            lists[d] = jnp.where(hit, lists[d + 1], lists[d])
        if live == len(lists) - 1:
            lists[live] = jnp.where(hit, -1.0, lists[live])
        if extra is not None:
            extra = jnp.where(extra == m, -1.0, extra)
    return vals


def _top_values(e, n):
    groups = [e[SUBLANE * r:SUBLANE * (r + 1)] for r in range(e.shape[0] // SUBLANE)]
    for a, b in _sorting_network(len(groups)):
        groups[a], groups[b] = jnp.maximum(groups[a], groups[b]), jnp.minimum(groups[a], groups[b])
    return jnp.concatenate(_merge_heads(groups[:n], None, n), axis=0)


def _peer_kernel(x1_ref, nffn_ref, wqt_ref, keys_ref, u_ref, vt_ref, nfin_ref, y_out,
                 xn_s, qt_s, e1_s, e2_s, th_s, acc_s, *slabs, te):
    tm = x1_ref.shape[0]
    ng = tm // LANE
    nb = te // LANE
    j = pl.program_id(1)

    @pl.when(j == 0)
    def _route():
        xn = (_rms(x1_ref[...]) * nffn_ref[...]).astype(BF16)
        xn_s[...] = xn
        qt_s[...] = _dot_nt(wqt_ref[...], xn).astype(BF16)
        acc_s[...] = jnp.zeros_like(acc_s)
        for h in range(PEER_HEADS):
            for p, e_s in ((0, e1_s), (1, e2_s)):
                r0 = (2 * h + p) * PEER_HALF
                s = _dot(keys_ref[p], qt_s[r0:r0 + PEER_HALF, :])
                e = jnp.exp(s - _col_max(s))
                for g in range(ng):
                    e_s[g, h] = e[:, g * LANE:(g + 1) * LANE]

        def route_one(g, h):
            e1 = e1_s[g, h]
            e2 = e2_s[g, h]
            v1 = _top_values(e1, PEER_TOPK)
            v2 = _top_values(e2, PEER_TOPK)
            half = PEER_TOPK // 2

            def products(w2):
                return ([v1[:half] * w2[b:b + 1] for b in range(PEER_TOPK)], v1[half:] * w2[0:1])

            cand = products(v2)
            th = _merge_heads(*cand, PEER_TOPK)[-1]
            sel = ([c >= th for c in cand[0]], cand[1] >= th)

            def selected(prod, fill, combine):
                out = jnp.where(sel[1], prod[1], fill)
                for s, c in zip(sel[0], prod[0]):
                    out = combine(out, jnp.where(s, c, fill))
                return out

            inv_z = 1.0 / jnp.sum(selected(cand, 0.0, jnp.add), axis=0, keepdims=True)
            e2_s[g, h] = e2 * inv_z
            thn = jnp.min(selected(products(v2 * inv_z), jnp.inf, jnp.minimum), axis=0, keepdims=True)
            th_s[g, h] = jnp.broadcast_to(thn, (SUBLANE, LANE))

        def route_group(i, _):
            for k in range(ROUTE_UNROLL):
                item = ROUTE_UNROLL * i + k
                route_one(item // PEER_HEADS, item % PEER_HEADS)
            return 0

        lax.fori_loop(0, ng * PEER_HEADS // ROUTE_UNROLL, route_group, 0)

    def hidden(c):
        slabs[2 * c][...] = _dot_nt(u_ref[c * 2 * LANE:(c + 1) * 2 * LANE, :], xn_s[...])

    def accumulate(c):
        acc_s[...] += _dot(vt_ref[0, :, c * 2 * LANE:(c + 1) * 2 * LANE], slabs[2 * c + 1][...])

    hidden(0)
    for c in range(nb // 2):
        ht_s, a_s = slabs[2 * c], slabs[2 * c + 1]
        if c + 1 < nb // 2:
            hidden(c + 1)
        for i in range(2):
            r = slice(i * LANE, (i + 1) * LANE)
            i1 = j * nb + 2 * c + i
            for g in range(ng):
                cols = slice(g * LANE, (g + 1) * LANE)
                w = jnp.zeros((LANE, LANE), F32)
                for h in range(PEER_HEADS):
                    z = e1_s[g, h, pl.ds(i1, 1), :] * e2_s[g, h]
                    w = w + jnp.where(z >= th_s[g, h, 0:1, :], z, 0.0)
                a_s[r, cols] = (w * _gelu(ht_s[r, cols])).astype(BF16)
        if c >= 1:
            accumulate(c - 1)
    accumulate(nb // 2 - 1)

    @pl.when(j == pl.num_programs(1) - 1)
    def _finish():
        x2 = x1_ref[...] + acc_s[...].T
        y_out[...] = _rms(x2) * nfin_ref[...]


def _peer(x1, w, tm):
    T, D = x1.shape
    NE = w["u"].shape[0]
    te = w["vt"].shape[2]
    assert T % tm == 0 and NE % te == 0 and tm % LANE == 0 and te % (2 * LANE) == 0
    ng = tm // LANE
    consts = [w["nffn"], w["wqt"], w["keys"]]
    return pl.pallas_call(
        functools.partial(_peer_kernel, te=te),
        grid=(T // tm, NE // te),
        in_specs=[pl.BlockSpec((tm, D), lambda i, j: (i, 0))]
        + [_const_spec(c.shape) for c in consts]
        + [pl.BlockSpec((te, D), lambda i, j: (j, 0)),
           pl.BlockSpec((1, D, te), lambda i, j: (j, 0, 0)),
           _const_spec(w["nfin"].shape)],
        out_specs=pl.BlockSpec((tm, D), lambda i, j: (i, 0)),
        out_shape=jax.ShapeDtypeStruct((T, D), F32),
        scratch_shapes=[
            pltpu.VMEM((tm, D), BF16),
            pltpu.VMEM((PEER_HEADS * 2 * PEER_HALF, tm), BF16),
            pltpu.VMEM((ng, PEER_HEADS, N_KEYS, LANE), F32),
            pltpu.VMEM((ng, PEER_HEADS, N_KEYS, LANE), F32),
            pltpu.VMEM((ng, PEER_HEADS, SUBLANE, LANE), F32),
            pltpu.VMEM((D, tm), F32),
        ] + [pltpu.VMEM((2 * LANE, tm), F32),
             pltpu.VMEM((2 * LANE, tm), BF16)] * (te // (2 * LANE)),
        compiler_params=pltpu.CompilerParams(
            dimension_semantics=("parallel", "arbitrary"), vmem_limit_bytes=VMEM_LIMIT),
        name="peer",
    )(x1, *consts, w["u"], w["vt"], w["nfin"])


def _rope_tables(seq):
    inv = 1.0 / (ROPE_THETA ** (jnp.arange(0, QK_ROPE, 2, dtype=F32) / QK_ROPE))
    ang = jnp.arange(seq, dtype=F32)[:, None] * inv[None, :]
    cos, sin = jnp.cos(ang), jnp.sin(ang)
    pad = jnp.zeros((seq, LANE - QK_ROPE), F32)
    return (jnp.concatenate([cos, cos, pad], axis=1), jnp.concatenate([sin, sin, pad], axis=1))


def _rot_cols(wpe):
    half = QK_ROPE // 2
    return jnp.concatenate([-wpe[..., half:], wpe[..., :half]], axis=-1)


def _prep_weights(norm_mix, w_in, q_norm, kv_norm, w_uq, w_ukv, w_o_b, sgu_norm, sgu_w, sgu_b,
                  w_o_a, w_out, norm_ffn, peer_wq, peer_keys, peer_u, peer_v, norm_final):
    a, b, c = Q_LORA, Q_LORA + KV_LORA, Q_LORA + KV_LORA + QK_ROPE
    d, e, f = c + SGU_WIDTH, c + 2 * SGU_WIDTH, c + 2 * SGU_WIDTH + D_MODEL
    zpad = lambda rows, n: jnp.zeros((rows, n), F32)
    w_kr = w_in[:, b:c]
    wkr = jnp.concatenate([w_kr, zpad(D_MODEL, LANE - QK_ROPE),
                           _rot_cols(w_kr), zpad(D_MODEL, LANE - QK_ROPE)], axis=1)
    uq = w_uq.reshape(Q_LORA, MLA_HEADS, QK_NOPE + QK_ROPE)
    hpad = jnp.zeros((Q_LORA, MLA_HEADS, LANE - QK_ROPE), F32)
    wuq = jnp.concatenate([uq, hpad], axis=-1).reshape(Q_LORA, MLA_HEADS * QK_PAD)
    wuqr = jnp.concatenate([_rot_cols(uq[..., QK_NOPE:]), hpad], axis=-1).reshape(Q_LORA, MLA_HEADS * LANE)
    ukv = w_ukv.reshape(KV_LORA, MLA_HEADS, QK_NOPE + V_HEAD)
    scale = (QK_NOPE + QK_ROPE) ** -0.5 * LOG2E
    sbias = jnp.repeat(sgu_b.T, SGU_WIDTH // SGU_GROUPS, axis=1)
    row = lambda v: v.reshape(1, -1).astype(F32)
    return {
        "nmix": row(norm_mix), "wq": w_in[:, :a].astype(BF16), "wkv": w_in[:, a:b].astype(BF16),
        "wkr": wkr.astype(BF16), "wu": w_in[:, c:d].astype(BF16), "wv": w_in[:, d:e].astype(BF16),
        "wga": w_in[:, e:f].astype(BF16), "wgb": w_in[:, f:].astype(BF16),
        "qn": row(q_norm) * scale, "kvn": row(kv_norm),
        "wuq": wuq.astype(BF16), "wuqr": wuqr.astype(BF16),
        "wuk": ukv[..., :QK_NOPE].reshape(KV_LORA, MLA_HEADS * QK_NOPE).astype(BF16),
        "wuvt": ukv[..., QK_NOPE:].reshape(KV_LORA, MLA_HEADS * V_HEAD).T.astype(BF16),
        "sgn": row(sgu_norm), "sw": sgu_w.astype(BF16), "sbias": sbias.astype(F32),
        "woa": w_o_a.astype(BF16), "wob": w_o_b.astype(BF16), "wout": w_out.astype(BF16),
        "nffn": row(norm_ffn), "wqt": peer_wq.T.astype(BF16), "keys": peer_keys.astype(BF16),
        "u": peer_u.astype(BF16), "nfin": row(norm_final),
        "vt": peer_v.reshape(-1, PEER_TE, D_MODEL).transpose(0, 2, 1).astype(BF16),
    }


def _trunk(x, w, front_tm=FRONT_TM, tq=ATTN_TQ, tk=ATTN_TK, post_tm=POST_TM,
           peer_tm=PEER_TM):
    B, S, D = x.shape
    cos, sin = _rope_tables(S)
    q, k, vt, ya, sgb = _front(x, cos, sin, w, front_tm)
    o = _attention(q, k, vt, tq, tk)
    flat = lambda t: t.reshape(B * S, t.shape[-1])
    x1 = _post(flat(x), flat(o), flat(ya), flat(sgb), w["wob"], w["wout"], post_tm)
    y = _peer(x1, w, peer_tm)
    return y.reshape(B, S, D)


def kernel(x_prompt, x_sample, norm_mix, w_in, q_norm, kv_norm, w_uq, w_ukv, w_o_b, sgu_norm,
           sgu_w, sgu_b, w_o_a, w_out, norm_ffn, peer_wq, peer_keys, peer_u, peer_v, norm_final):
    assert norm_mix.shape[0] == 1, "single layer"
    w = _prep_weights(norm_mix[0], w_in[0], q_norm[0], kv_norm[0], w_uq[0], w_ukv[0], w_o_b[0],
                      sgu_norm[0], sgu_w[0], sgu_b[0], w_o_a[0], w_out[0], norm_ffn[0],
                      peer_wq[0], peer_keys[0], peer_u[0], peer_v[0], norm_final)
    return (_trunk(x_prompt, w), _trunk(x_sample, w))
```

```python
import functools

import jax
import jax.numpy as jnp
from jax import lax
from jax.experimental import pallas as pl
from jax.experimental.pallas import tpu as pltpu

F32 = jnp.float32
BF16 = jnp.bfloat16

D_MODEL = 1024
SGU_CHUNK = 128
SGU_GROUPS = 8
SGU_WIDTH = 1024
MLA_HEADS = 8
QK_NOPE = 128
QK_ROPE = 64
V_HEAD = 128
Q_LORA = 384
KV_LORA = 256
ROPE_THETA = 10000.0
PEER_HEADS = 8
N_KEYS = 128
PEER_HALF = 128
PEER_TOPK = 16
EPS = 1e-6

LANE = 128
SUBLANE = 8
QK_PAD = 256
LOG2E = 1.4426950408889634
VMEM_LIMIT = 56 * 1024 * 1024

FRONT_TM = 512
ATTN_TQ = 1024
ATTN_TK = 1024
ATTN_UNROLL = 4
POST_TM = 1024
PEER_TM = 512
PEER_TE = 2048
ACC_SLABS = 4
ROUTE_UNROLL = 8


def _rms(x, eps=EPS):
    return x * lax.rsqrt(jnp.mean(x * x, axis=-1, keepdims=True) + eps)


_GELU_K0 = -2.0 * 0.7978845608028654 * LOG2E
_GELU_K1 = _GELU_K0 * 0.044715


def _gelu(x):
    return x / (1.0 + jnp.exp2(x * (_GELU_K0 + _GELU_K1 * (x * x))))


def _dot(a, b):
    return jnp.dot(a, b, preferred_element_type=F32)


def _dot_nt(a, b):
    return lax.dot_general(a, b, (((1,), (1,)), ((), ())), preferred_element_type=F32)


def _const_spec(shape):
    nd = len(shape)
    return pl.BlockSpec(shape, lambda *_: (0,) * nd, pipeline_mode=pl.Buffered(1))


def _front_kernel(x_ref, cos_ref, sin_ref, nmix_ref, wq_ref, wkv_ref, wkr_ref, wu_ref, wv_ref,
                  wga_ref, wgb_ref, qn_ref, kvn_ref, wuq_ref, wuqr_ref, wuk_ref, wuvt_ref,
                  sgn_ref, sw_ref, sbias_ref, woa_ref,
                  q_out, k_out, vt_out, ya_out, sgb_out, vn_s, sgu_s):
    tm = x_ref.shape[1]
    x = x_ref[0]
    xn = (_rms(x) * nmix_ref[...]).astype(BF16)
    cos = cos_ref[...]
    sin = sin_ref[...]

    c_q = (_rms(_dot(xn, wq_ref[...])) * qn_ref[...]).astype(BF16)
    qa = _dot(c_q, wuq_ref[...])
    qr = _dot(c_q, wuqr_ref[...])
    for h in range(MLA_HEADS):
        q_out[0, :, h * QK_PAD:h * QK_PAD + LANE] = qa[:, h * QK_PAD:h * QK_PAD + LANE].astype(BF16)
        pe = qa[:, h * QK_PAD + LANE:(h + 1) * QK_PAD] * cos + qr[:, h * LANE:(h + 1) * LANE] * sin
        q_out[0, :, h * QK_PAD + LANE:(h + 1) * QK_PAD] = pe.astype(BF16)

    c_kv = (_rms(_dot(xn, wkv_ref[...])) * kvn_ref[...]).astype(BF16)
    kn = _dot(c_kv, wuk_ref[...])
    kr = _dot(xn, wkr_ref[...])
    kpe = (kr[:, :LANE] * cos + kr[:, LANE:] * sin).astype(BF16)
    for h in range(MLA_HEADS):
        k_out[0, :, h * QK_PAD:h * QK_PAD + LANE] = kn[:, h * LANE:(h + 1) * LANE].astype(BF16)
        k_out[0, :, h * QK_PAD + LANE:(h + 1) * QK_PAD] = kpe
    vt_out[0] = _dot_nt(wuvt_ref[...], c_kv).astype(BF16)

    v = _gelu(_dot(xn, wv_ref[...]))
    vn_s[...] = (_rms(v) * sgn_ref[...]).astype(BF16)
    u = _gelu(_dot(xn, wu_ref[...]))
    for c in range(tm // SGU_CHUNK):
        rows = slice(c * SGU_CHUNK, (c + 1) * SGU_CHUNK)
        for g in range(SGU_GROUPS):
            cols = slice(g * LANE, (g + 1) * LANE)
            mixed = _dot(sw_ref[g], vn_s[rows, cols]) + sbias_ref[:, cols]
            sgu_s[rows, cols] = (u[rows, cols] * mixed).astype(BF16)
    ya = _dot(sgu_s[...], woa_ref[...])
    ya_out[0] = (jax.nn.sigmoid(_dot(xn, wga_ref[...])) * ya).astype(BF16)
    sgb_out[0] = jax.nn.sigmoid(_dot(xn, wgb_ref[...])).astype(BF16)


def _front(x, cos, sin, w, tm):
    B, S, D = x.shape
    assert S % tm == 0 and tm % SGU_CHUNK == 0
    nt = S // tm
    tok = lambda width: pl.BlockSpec((1, tm, width), lambda b, i: (b, i, 0))
    pos = pl.BlockSpec((tm, LANE), lambda b, i: (i, 0))
    consts = [w["nmix"], w["wq"], w["wkv"], w["wkr"], w["wu"], w["wv"], w["wga"], w["wgb"],
              w["qn"], w["kvn"], w["wuq"], w["wuqr"], w["wuk"], w["wuvt"], w["sgn"], w["sw"],
              w["sbias"], w["woa"]]
    out_shape = (
        jax.ShapeDtypeStruct((B, S, MLA_HEADS * QK_PAD), BF16),
        jax.ShapeDtypeStruct((B, S, MLA_HEADS * QK_PAD), BF16),
        jax.ShapeDtypeStruct((B, MLA_HEADS * V_HEAD, S), BF16),
        jax.ShapeDtypeStruct((B, S, D), BF16),
        jax.ShapeDtypeStruct((B, S, D), BF16),
    )
    return pl.pallas_call(
        _front_kernel,
        grid=(B, nt),
        in_specs=[tok(D), pos, pos] + [_const_spec(c.shape) for c in consts],
        out_specs=(tok(MLA_HEADS * QK_PAD), tok(MLA_HEADS * QK_PAD),
                   pl.BlockSpec((1, MLA_HEADS * V_HEAD, tm), lambda b, i: (b, 0, i)),
                   tok(D), tok(D)),
        out_shape=out_shape,
        scratch_shapes=[pltpu.VMEM((tm, SGU_WIDTH), BF16), pltpu.VMEM((tm, SGU_WIDTH), BF16)],
        compiler_params=pltpu.CompilerParams(
            dimension_semantics=("parallel", "parallel"), vmem_limit_bytes=VMEM_LIMIT),
        name="front",
    )(x, cos, sin, *consts)


def _attn_kernel(q_ref, k_ref, vt_ref, o_ref, sa_s, sb_s, *, tk):
    tq = q_ref.shape[1]
    nk = k_ref.shape[1] // tk
    q = q_ref[0]

    def scores(s_ref, c):
        start = pl.multiple_of(c * tk, tk)
        s = _dot_nt(k_ref[0, pl.ds(start, tk), :], q)
        s_ref[...] = s
        return jnp.max(s, axis=0, keepdims=True)

    def update(s_ref, mc, c, carry):
        m, l, acc = carry
        start = pl.multiple_of(c * tk, tk)
        m_new = jnp.maximum(m, mc)
        alpha = jnp.exp2(m - m_new)
        p = jnp.exp2(s_ref[...] - m_new)
        l = alpha * l + jnp.sum(p, axis=0, keepdims=True)
        acc = alpha * acc + _dot(vt_ref[0, :, pl.ds(start, tk)], p.astype(BF16))
        return m_new, l, acc

    def run(c0, n, mc, state, more):
        bufs = (sa_s, sb_s)
        for t in range(n):
            mc_next = scores(bufs[(t + 1) % 2], c0 + t + 1) if (t + 1 < n or more) else None
            state = update(bufs[t % 2], mc, c0 + t, state)
            mc = mc_next
        return mc, state

    unroll = ATTN_UNROLL if nk >= 4 * ATTN_UNROLL else 2

    def body(i, carry):
        return run(i * unroll, unroll, *carry, more=True)

    state = (jnp.full((1, tq), -jnp.inf, F32), jnp.zeros((1, tq), F32), jnp.zeros((V_HEAD, tq), F32))
    nloop = nk // unroll - 1
    mc, state = lax.fori_loop(0, nloop, body, (scores(sa_s, 0), state))
    _, (_, l, acc) = run(nloop * unroll, unroll, mc, state, more=False)
    o_ref[0] = (acc / l).T.astype(o_ref.dtype)


def _attention(q, k, vt, tq, tk):
    B, S, _ = q.shape
    assert S % tq == 0 and S % (ATTN_UNROLL * tk) == 0 and ATTN_UNROLL % 4 == 0
    return pl.pallas_call(
        functools.partial(_attn_kernel, tk=tk),
        scratch_shapes=[pltpu.VMEM((tk, tq), F32)] * 2,
        grid=(B, MLA_HEADS, S // tq),
        in_specs=[
            pl.BlockSpec((1, tq, QK_PAD), lambda b, h, i: (b, i, h)),
            pl.BlockSpec((1, S, QK_PAD), lambda b, h, i: (b, 0, h)),
            pl.BlockSpec((1, V_HEAD, S), lambda b, h, i: (b, h, 0)),
        ],
        out_specs=pl.BlockSpec((1, tq, V_HEAD), lambda b, h, i: (b, i, h)),
        out_shape=jax.ShapeDtypeStruct((B, S, MLA_HEADS * V_HEAD), BF16),
        compiler_params=pltpu.CompilerParams(
            dimension_semantics=("parallel", "parallel", "arbitrary"),
            vmem_limit_bytes=VMEM_LIMIT),
        name="attn",
    )(q, k, vt)


def _post_kernel(x_ref, o_ref, ya_ref, sgb_ref, wob_ref, wout_ref, x1_out):
    yb = _dot(o_ref[...], wob_ref[...])
    merged = ya_ref[...].astype(F32) + sgb_ref[...].astype(F32) * yb
    x1_out[...] = x_ref[...] + _dot(merged.astype(BF16), wout_ref[...])


def _post(x2, o2, ya2, sgb2, wob, wout, tm):
    T, D = x2.shape
    assert T % tm == 0
    tok = pl.BlockSpec((tm, D), lambda i: (i, 0))
    return pl.pallas_call(
        _post_kernel,
        grid=(T // tm,),
        in_specs=[tok, tok, tok, tok, _const_spec(wob.shape), _const_spec(wout.shape)],
        out_specs=tok,
        out_shape=jax.ShapeDtypeStruct((T, D), F32),
        compiler_params=pltpu.CompilerParams(
            dimension_semantics=("parallel",), vmem_limit_bytes=VMEM_LIMIT),
        name="post",
    )(x2, o2, ya2, sgb2, wob, wout)


def _col_max(a):
    return jnp.max(a, axis=0, keepdims=True)


def _sorting_network(n):
    comps = []
    p = 1
    while p < n:
        k = p
        while k >= 1:
            for j in range(k % p, n - k, 2 * k):
                for i in range(min(k, n - j - k)):
                    if (i + j) // (2 * p) == (i + j + k) // (2 * p):
                        comps.append((i + j, i + j + k))
            k //= 2
        p *= 2
    return comps


def _merge_heads(lists, extra, n):
    lists = list(lists)
    vals = []
    for t in range(n):
        head = lists[0] if extra is None else jnp.maximum(lists[0], extra)
        m = jnp.maximum(_col_max(head), 0.0)
        vals.append(m)
        if t == n - 1:
            break
        hit = lists[0] == m
        live = min(len(lists) - 1, n - 1 - t)
        for d in range(live):
            lists[d] = jnp.where(hit, lists[d + 1], lists[d])
        if live == len(lists) - 1:
            lists[live] = jnp.where(hit, -1.0, lists[live])
        if extra is not None:
            extra = jnp.where(extra == m, -1.0, extra)
    return vals


def _top_values(e, n):
    groups = [e[SUBLANE * r:SUBLANE * (r + 1)] for r in range(e.shape[0] // SUBLANE)]
    for a, b in _sorting_network(len(groups)):
        groups[a], groups[b] = jnp.maximum(groups[a], groups[b]), jnp.minimum(groups[a], groups[b])
    return jnp.concatenate(_merge_heads(groups[:n], None, n), axis=0)


def _peer_kernel(x1_ref, nffn_ref, wqt_ref, keys_ref, u_ref, vt_ref, nfin_ref, y_out,
                 xn_s, qt_s, e1_s, e2_s, th_s, acc_s, *slabs, te):
    tm = x1_ref.shape[0]
    ng = tm // LANE
    nb = te // LANE
    j = pl.program_id(1)

    @pl.when(j == 0)
    def _route():
        xn = (_rms(x1_ref[...]) * nffn_ref[...]).astype(BF16)
        xn_s[...] = xn
        qt_s[...] = _dot_nt(wqt_ref[...], xn).astype(BF16)
        acc_s[...] = jnp.zeros_like(acc_s)
        for h in range(PEER_HEADS):
            for p, e_s in ((0, e1_s), (1, e2_s)):
                r0 = (2 * h + p) * PEER_HALF
                s = _dot(keys_ref[p], qt_s[r0:r0 + PEER_HALF, :])
                e = jnp.exp(s - _col_max(s))
                for g in range(ng):
                    e_s[g, h] = e[:, g * LANE:(g + 1) * LANE]

        def route_one(g, h):
            e1 = e1_s[g, h]
            e2 = e2_s[g, h]
            v1 = _top_values(e1, PEER_TOPK)
            v2 = _top_values(e2, PEER_TOPK)
            half = PEER_TOPK // 2

            def products(w2):
                return ([v1[:half] * w2[b:b + 1] for b in range(PEER_TOPK)], v1[half:] * w2[0:1])

            cand = products(v2)
            th = _merge_heads(*cand, PEER_TOPK)[-1]
            sel = ([c >= th for c in cand[0]], cand[1] >= th)

            def selected(prod, fill, combine):
                out = jnp.where(sel[1], prod[1], fill)
                for s, c in zip(sel[0], prod[0]):
                    out = combine(out, jnp.where(s, c, fill))
                return out

            inv_z = 1.0 / jnp.sum(selected(cand, 0.0, jnp.add), axis=0, keepdims=True)
            e2_s[g, h] = e2 * inv_z
            thn = jnp.min(selected(products(v2 * inv_z), jnp.inf, jnp.minimum), axis=0, keepdims=True)
            th_s[g, h] = jnp.broadcast_to(thn, (SUBLANE, LANE))

        def route_group(i, _):
            for k in range(ROUTE_UNROLL):
                item = ROUTE_UNROLL * i + k
                route_one(item // PEER_HEADS, item % PEER_HEADS)
            return 0

        lax.fori_loop(0, ng * PEER_HEADS // ROUTE_UNROLL, route_group, 0)

    nslab = nb // 2
    ht_bufs, a_bufs = slabs[:nslab], slabs[nslab:]
    group = ACC_SLABS * 2 * LANE

    def hidden(c):
        ht_bufs[c][...] = _dot_nt(u_ref[c * 2 * LANE:(c + 1) * 2 * LANE, :], xn_s[...])

    def accumulate(p):
        acc_s[...] += _dot(vt_ref[0, :, p * group:(p + 1) * group], a_bufs[p][...])

    hidden(0)
    for c in range(nslab):
        ht_s, a_s = ht_bufs[c], a_bufs[c // ACC_SLABS]
        if c + 1 < nslab:
            hidden(c + 1)
        for i in range(2):
            r = slice(i * LANE, (i + 1) * LANE)
            k = 2 * (c % ACC_SLABS) + i
            ra = slice(k * LANE, (k + 1) * LANE)
            i1 = j * nb + 2 * c + i
            for g in range(ng):
                cols = slice(g * LANE, (g + 1) * LANE)
                w = jnp.zeros((LANE, LANE), F32)
                for h in range(PEER_HEADS):
                    z = e1_s[g, h, pl.ds(i1, 1), :] * e2_s[g, h]
                    w = w + jnp.where(z >= th_s[g, h, 0:1, :], z, 0.0)
                a_s[ra, cols] = (w * _gelu(ht_s[r, cols])).astype(BF16)
        if c % ACC_SLABS == 0 and c >= ACC_SLABS:
            accumulate(c // ACC_SLABS - 1)
    accumulate(nslab // ACC_SLABS - 1)

    @pl.when(j == pl.num_programs(1) - 1)
    def _finish():
        x2 = x1_ref[...] + acc_s[...].T
        y_out[...] = _rms(x2) * nfin_ref[...]


def _peer(x1, w, tm):
    T, D = x1.shape
    NE = w["u"].shape[0]
    te = w["vt"].shape[2]
    assert T % tm == 0 and NE % te == 0 and tm % LANE == 0 and te % (2 * LANE) == 0
    ng = tm // LANE
    consts = [w["nffn"], w["wqt"], w["keys"]]
    return pl.pallas_call(
        functools.partial(_peer_kernel, te=te),
        grid=(T // tm, NE // te),
        in_specs=[pl.BlockSpec((tm, D), lambda i, j: (i, 0))]
        + [_const_spec(c.shape) for c in consts]
        + [pl.BlockSpec((te, D), lambda i, j: (j, 0)),
           pl.BlockSpec((1, D, te), lambda i, j: (j, 0, 0)),
           _const_spec(w["nfin"].shape)],
        out_specs=pl.BlockSpec((tm, D), lambda i, j: (i, 0)),
        out_shape=jax.ShapeDtypeStruct((T, D), F32),
        scratch_shapes=[
            pltpu.VMEM((tm, D), BF16),
            pltpu.VMEM((PEER_HEADS * 2 * PEER_HALF, tm), BF16),
            pltpu.VMEM((ng, PEER_HEADS, N_KEYS, LANE), F32),
            pltpu.VMEM((ng, PEER_HEADS, N_KEYS, LANE), F32),
            pltpu.VMEM((ng, PEER_HEADS, SUBLANE, LANE), F32),
            pltpu.VMEM((D, tm), F32),
        ] + [pltpu.VMEM((2 * LANE, tm), F32)] * (te // (2 * LANE))
        + [pltpu.VMEM((ACC_SLABS * 2 * LANE, tm), BF16)] * (te // (ACC_SLABS * 2 * LANE)),
        compiler_params=pltpu.CompilerParams(
            dimension_semantics=("parallel", "arbitrary"), vmem_limit_bytes=VMEM_LIMIT),
        name="peer",
    )(x1, *consts, w["u"], w["vt"], w["nfin"])


def _rope_tables(seq):
    inv = 1.0 / (ROPE_THETA ** (jnp.arange(0, QK_ROPE, 2, dtype=F32) / QK_ROPE))
    ang = jnp.arange(seq, dtype=F32)[:, None] * inv[None, :]
    cos, sin = jnp.cos(ang), jnp.sin(ang)
    pad = jnp.zeros((seq, LANE - QK_ROPE), F32)
    return (jnp.concatenate([cos, cos, pad], axis=1), jnp.concatenate([sin, sin, pad], axis=1))


def _rot_cols(wpe):
    half = QK_ROPE // 2
    return jnp.concatenate([-wpe[..., half:], wpe[..., :half]], axis=-1)


def _prep_weights(norm_mix, w_in, q_norm, kv_norm, w_uq, w_ukv, w_o_b, sgu_norm, sgu_w, sgu_b,
                  w_o_a, w_out, norm_ffn, peer_wq, peer_keys, peer_u, peer_v, norm_final):
    a, b, c = Q_LORA, Q_LORA + KV_LORA, Q_LORA + KV_LORA + QK_ROPE
    d, e, f = c + SGU_WIDTH, c + 2 * SGU_WIDTH, c + 2 * SGU_WIDTH + D_MODEL
    zpad = lambda rows, n: jnp.zeros((rows, n), F32)
    w_kr = w_in[:, b:c]
    wkr = jnp.concatenate([w_kr, zpad(D_MODEL, LANE - QK_ROPE),
                           _rot_cols(w_kr), zpad(D_MODEL, LANE - QK_ROPE)], axis=1)
    uq = w_uq.reshape(Q_LORA, MLA_HEADS, QK_NOPE + QK_ROPE)
    hpad = jnp.zeros((Q_LORA, MLA_HEADS, LANE - QK_ROPE), F32)
    wuq = jnp.concatenate([uq, hpad], axis=-1).reshape(Q_LORA, MLA_HEADS * QK_PAD)
    wuqr = jnp.concatenate([_rot_cols(uq[..., QK_NOPE:]), hpad], axis=-1).reshape(Q_LORA, MLA_HEADS * LANE)
    ukv = w_ukv.reshape(KV_LORA, MLA_HEADS, QK_NOPE + V_HEAD)
    scale = (QK_NOPE + QK_ROPE) ** -0.5 * LOG2E
    sbias = jnp.repeat(sgu_b.T, SGU_WIDTH // SGU_GROUPS, axis=1)
    row = lambda v: v.reshape(1, -1).astype(F32)
    return {
        "nmix": row(norm_mix), "wq": w_in[:, :a].astype(BF16), "wkv": w_in[:, a:b].astype(BF16),
        "wkr": wkr.astype(BF16), "wu": w_in[:, c:d].astype(BF16), "wv": w_in[:, d:e].astype(BF16),
        "wga": w_in[:, e:f].astype(BF16), "wgb": w_in[:, f:].astype(BF16),
        "qn": row(q_norm) * scale, "kvn": row(kv_norm),
        "wuq": wuq.astype(BF16), "wuqr": wuqr.astype(BF16),
        "wuk": ukv[..., :QK_NOPE].reshape(KV_LORA, MLA_HEADS * QK_NOPE).astype(BF16),
        "wuvt": ukv[..., QK_NOPE:].reshape(KV_LORA, MLA_HEADS * V_HEAD).T.astype(BF16),
        "sgn": row(sgu_norm), "sw": sgu_w.astype(BF16), "sbias": sbias.astype(F32),
        "woa": w_o_a.astype(BF16), "wob": w_o_b.astype(BF16), "wout": w_out.astype(BF16),
        "nffn": row(norm_ffn), "wqt": peer_wq.T.astype(BF16), "keys": peer_keys.astype(BF16),
        "u": peer_u.astype(BF16), "nfin": row(norm_final),
        "vt": peer_v.reshape(-1, PEER_TE, D_MODEL).transpose(0, 2, 1).astype(BF16),
    }


def _trunk(x, w, front_tm=FRONT_TM, tq=ATTN_TQ, tk=ATTN_TK, post_tm=POST_TM,
           peer_tm=PEER_TM):
    B, S, D = x.shape
    cos, sin = _rope_tables(S)
    q, k, vt, ya, sgb = _front(x, cos, sin, w, front_tm)
    o = _attention(q, k, vt, tq, tk)
    flat = lambda t: t.reshape(B * S, t.shape[-1])
    x1 = _post(flat(x), flat(o), flat(ya), flat(sgb), w["wob"], w["wout"], post_tm)
    y = _peer(x1, w, peer_tm)
    return y.reshape(B, S, D)


def kernel(x_prompt, x_sample, norm_mix, w_in, q_norm, kv_norm, w_uq, w_ukv, w_o_b, sgu_norm,
           sgu_w, sgu_b, w_o_a, w_out, norm_ffn, peer_wq, peer_keys, peer_u, peer_v, norm_final):
    assert norm_mix.shape[0] == 1, "single layer"
    w = _prep_weights(norm_mix[0], w_in[0], q_norm[0], kv_norm[0], w_uq[0], w_ukv[0], w_o_b[0],
                      sgu_norm[0], sgu_w[0], sgu_b[0], w_o_a[0], w_out[0], norm_ffn[0],
                      peer_wq[0], peer_keys[0], peer_u[0], peer_v[0], norm_final)
    return (_trunk(x_prompt, w), _trunk(x_sample, w))
```

```python
import functools

import jax
import jax.numpy as jnp
from jax import lax
from jax.experimental import pallas as pl
from jax.experimental.pallas import tpu as pltpu

F32 = jnp.float32
BF16 = jnp.bfloat16

D_MODEL = 1024
SGU_CHUNK = 128
SGU_GROUPS = 8
SGU_WIDTH = 1024
MLA_HEADS = 8
QK_NOPE = 128
QK_ROPE = 64
V_HEAD = 128
Q_LORA = 384
KV_LORA = 256
ROPE_THETA = 10000.0
PEER_HEADS = 8
N_KEYS = 128
PEER_HALF = 128
PEER_TOPK = 16
EPS = 1e-6

LANE = 128
SUBLANE = 8
QK_PAD = 256
LOG2E = 1.4426950408889634
VMEM_LIMIT = 56 * 1024 * 1024

FRONT_TM = 512
ATTN_TQ = 1024
ATTN_TK = 1024
ATTN_UNROLL = 4
POST_TM = 1024
PEER_TM = 512
PEER_TE = 2048
ROUTE_UNROLL = 4


def _rms(x, eps=EPS):
    return x * lax.rsqrt(jnp.mean(x * x, axis=-1, keepdims=True) + eps)


_GELU_K0 = -2.0 * 0.7978845608028654 * LOG2E
_GELU_K1 = _GELU_K0 * 0.044715


def _gelu(x):
    return x / (1.0 + jnp.exp2(x * (_GELU_K0 + _GELU_K1 * (x * x))))


def _dot(a, b):
    return jnp.dot(a, b, preferred_element_type=F32)


def _dot_nt(a, b):
    return lax.dot_general(a, b, (((1,), (1,)), ((), ())), preferred_element_type=F32)


def _const_spec(shape):
    nd = len(shape)
    return pl.BlockSpec(shape, lambda *_: (0,) * nd, pipeline_mode=pl.Buffered(1))


def _front_kernel(x_ref, cos_ref, sin_ref, nmix_ref, wq_ref, wkv_ref, wkr_ref, wu_ref, wv_ref,
                  wga_ref, wgb_ref, qn_ref, kvn_ref, wuq_ref, wuqr_ref, wuk_ref, wuvt_ref,
                  sgn_ref, sw_ref, sbias_ref, woa_ref,
                  q_out, k_out, vt_out, ya_out, sgb_out, vn_s, sgu_s):
    tm = x_ref.shape[1]
    x = x_ref[0]
    xn = (_rms(x) * nmix_ref[...]).astype(BF16)
    cos = cos_ref[...]
    sin = sin_ref[...]

    c_q = (_rms(_dot(xn, wq_ref[...])) * qn_ref[...]).astype(BF16)
    qa = _dot(c_q, wuq_ref[...])
    qr = _dot(c_q, wuqr_ref[...])
    for h in range(MLA_HEADS):
        q_out[0, :, h * QK_PAD:h * QK_PAD + LANE] = qa[:, h * QK_PAD:h * QK_PAD + LANE].astype(BF16)
        pe = qa[:, h * QK_PAD + LANE:(h + 1) * QK_PAD] * cos + qr[:, h * LANE:(h + 1) * LANE] * sin
        q_out[0, :, h * QK_PAD + LANE:(h + 1) * QK_PAD] = pe.astype(BF16)

    c_kv = (_rms(_dot(xn, wkv_ref[...])) * kvn_ref[...]).astype(BF16)
    kn = _dot(c_kv, wuk_ref[...])
    kr = _dot(xn, wkr_ref[...])
    kpe = (kr[:, :LANE] * cos + kr[:, LANE:] * sin).astype(BF16)
    for h in range(MLA_HEADS):
        k_out[0, :, h * QK_PAD:h * QK_PAD + LANE] = kn[:, h * LANE:(h + 1) * LANE].astype(BF16)
        k_out[0, :, h * QK_PAD + LANE:(h + 1) * QK_PAD] = kpe
    vt_out[0] = _dot_nt(wuvt_ref[...], c_kv).astype(BF16)

    v = _gelu(_dot(xn, wv_ref[...]))
    vn_s[...] = (_rms(v) * sgn_ref[...]).astype(BF16)
    u = _gelu(_dot(xn, wu_ref[...]))
    for c in range(tm // SGU_CHUNK):
        rows = slice(c * SGU_CHUNK, (c + 1) * SGU_CHUNK)
        for g in range(SGU_GROUPS):
            cols = slice(g * LANE, (g + 1) * LANE)
            mixed = _dot(sw_ref[g], vn_s[rows, cols]) + sbias_ref[:, cols]
            sgu_s[rows, cols] = (u[rows, cols] * mixed).astype(BF16)
    ya = _dot(sgu_s[...], woa_ref[...])
    ya_out[0] = (jax.nn.sigmoid(_dot(xn, wga_ref[...])) * ya).astype(BF16)
    sgb_out[0] = jax.nn.sigmoid(_dot(xn, wgb_ref[...])).astype(BF16)


def _front(x, cos, sin, w, tm):
    B, S, D = x.shape
    assert S % tm == 0 and tm % SGU_CHUNK == 0
    nt = S // tm
    tok = lambda width: pl.BlockSpec((1, tm, width), lambda b, i: (b, i, 0))
    pos = pl.BlockSpec((tm, LANE), lambda b, i: (i, 0))
    consts = [w["nmix"], w["wq"], w["wkv"], w["wkr"], w["wu"], w["wv"], w["wga"], w["wgb"],
              w["qn"], w["kvn"], w["wuq"], w["wuqr"], w["wuk"], w["wuvt"], w["sgn"], w["sw"],
              w["sbias"], w["woa"]]
    out_shape = (
        jax.ShapeDtypeStruct((B, S, MLA_HEADS * QK_PAD), BF16),
        jax.ShapeDtypeStruct((B, S, MLA_HEADS * QK_PAD), BF16),
        jax.ShapeDtypeStruct((B, MLA_HEADS * V_HEAD, S), BF16),
        jax.ShapeDtypeStruct((B, S, D), BF16),
        jax.ShapeDtypeStruct((B, S, D), BF16),
    )
    return pl.pallas_call(
        _front_kernel,
        grid=(B, nt),
        in_specs=[tok(D), pos, pos] + [_const_spec(c.shape) for c in consts],
        out_specs=(tok(MLA_HEADS * QK_PAD), tok(MLA_HEADS * QK_PAD),
                   pl.BlockSpec((1, MLA_HEADS * V_HEAD, tm), lambda b, i: (b, 0, i)),
                   tok(D), tok(D)),
        out_shape=out_shape,
        scratch_shapes=[pltpu.VMEM((tm, SGU_WIDTH), BF16), pltpu.VMEM((tm, SGU_WIDTH), BF16)],
        compiler_params=pltpu.CompilerParams(
            dimension_semantics=("parallel", "parallel"), vmem_limit_bytes=VMEM_LIMIT),
        name="front",
    )(x, cos, sin, *consts)


def _attn_kernel(q_ref, k_ref, vt_ref, o_ref, sa_s, sb_s, *, tk):
    tq = q_ref.shape[1]
    nk = k_ref.shape[1] // tk
    q = q_ref[0]

    def scores(s_ref, c):
        start = pl.multiple_of(c * tk, tk)
        s = _dot_nt(k_ref[0, pl.ds(start, tk), :], q)
        s_ref[...] = s
        return jnp.max(s, axis=0, keepdims=True)

    def update(s_ref, mc, c, carry):
        m, l, acc = carry
        start = pl.multiple_of(c * tk, tk)
        m_new = jnp.maximum(m, mc)
        alpha = jnp.exp2(m - m_new)
        p = jnp.exp2(s_ref[...] - m_new)
        l = alpha * l + jnp.sum(p, axis=0, keepdims=True)
        acc = alpha * acc + _dot(vt_ref[0, :, pl.ds(start, tk)], p.astype(BF16))
        return m_new, l, acc

    def run(c0, n, mc, state, more):
        bufs = (sa_s, sb_s)
        for t in range(n):
            mc_next = scores(bufs[(t + 1) % 2], c0 + t + 1) if (t + 1 < n or more) else None
            state = update(bufs[t % 2], mc, c0 + t, state)
            mc = mc_next
        return mc, state

    unroll = ATTN_UNROLL if nk >= 4 * ATTN_UNROLL else 2

    def body(i, carry):
        return run(i * unroll, unroll, *carry, more=True)

    state = (jnp.full((1, tq), -jnp.inf, F32), jnp.zeros((1, tq), F32), jnp.zeros((V_HEAD, tq), F32))
    nloop = nk // unroll - 1
    mc, state = lax.fori_loop(0, nloop, body, (scores(sa_s, 0), state))
    _, (_, l, acc) = run(nloop * unroll, unroll, mc, state, more=False)
    o_ref[0] = (acc / l).T.astype(o_ref.dtype)


def _attention(q, k, vt, tq, tk):
    B, S, _ = q.shape
    assert S % tq == 0 and S % (ATTN_UNROLL * tk) == 0 and ATTN_UNROLL % 4 == 0
    return pl.pallas_call(
        functools.partial(_attn_kernel, tk=tk),
        scratch_shapes=[pltpu.VMEM((tk, tq), F32)] * 2,
        grid=(B, MLA_HEADS, S // tq),
        in_specs=[
            pl.BlockSpec((1, tq, QK_PAD), lambda b, h, i: (b, i, h)),
            pl.BlockSpec((1, S, QK_PAD), lambda b, h, i: (b, 0, h)),
            pl.BlockSpec((1, V_HEAD, S), lambda b, h, i: (b, h, 0)),
        ],
        out_specs=pl.BlockSpec((1, tq, V_HEAD), lambda b, h, i: (b, i, h)),
        out_shape=jax.ShapeDtypeStruct((B, S, MLA_HEADS * V_HEAD), BF16),
        compiler_params=pltpu.CompilerParams(
            dimension_semantics=("parallel", "parallel", "arbitrary"),
            vmem_limit_bytes=VMEM_LIMIT),
        name="attn",
    )(q, k, vt)


def _post_kernel(x_ref, o_ref, ya_ref, sgb_ref, wob_ref, wout_ref, x1_out):
    yb = _dot(o_ref[...], wob_ref[...])
    merged = ya_ref[...].astype(F32) + sgb_ref[...].astype(F32) * yb
    x1_out[...] = x_ref[...] + _dot(merged.astype(BF16), wout_ref[...])


def _post(x2, o2, ya2, sgb2, wob, wout, tm):
    T, D = x2.shape
    assert T % tm == 0
    tok = pl.BlockSpec((tm, D), lambda i: (i, 0))
    return pl.pallas_call(
        _post_kernel,
        grid=(T // tm,),
        in_specs=[tok, tok, tok, tok, _const_spec(wob.shape), _const_spec(wout.shape)],
        out_specs=tok,
        out_shape=jax.ShapeDtypeStruct((T, D), F32),
        compiler_params=pltpu.CompilerParams(
            dimension_semantics=("parallel",), vmem_limit_bytes=VMEM_LIMIT),
        name="post",
    )(x2, o2, ya2, sgb2, wob, wout)


def _col_max(a):
    return jnp.max(a, axis=0, keepdims=True)


def _sorting_network(n):
    comps = []
    p = 1
    while p < n:
        k = p
        while k >= 1:
            for j in range(k % p, n - k, 2 * k):
                for i in range(min(k, n - j - k)):
                    if (i + j) // (2 * p) == (i + j + k) // (2 * p):
                        comps.append((i + j, i + j + k))
            k //= 2
        p *= 2
    return comps


def _merge_heads(lists, extra, n):
    lists = list(lists)
    vals = []
    for t in range(n):
        head = lists[0] if extra is None else jnp.maximum(lists[0], extra)
        m = jnp.maximum(_col_max(head), 0.0)
        vals.append(m)
        if t == n - 1:
            break
        hit = lists[0] == m
        live = min(len(lists) - 1, n - 1 - t)
        for d in range(live):
            lists[d] = jnp.where(hit, lists[d + 1], lists[d])
        if live == len(lists) - 1:
            lists[live] = jnp.where(hit, -1.0, lists[live])
        if extra is not None:
            extra = jnp.where(extra == m, -1.0, extra)
    return vals


def _top_values(e, n):
    groups = [e[SUBLANE * r:SUBLANE * (r + 1)] for r in range(e.shape[0] // SUBLANE)]
    for a, b in _sorting_network(len(groups)):
        groups[a], groups[b] = jnp.maximum(groups[a], groups[b]), jnp.minimum(groups[a], groups[b])
    return jnp.concatenate(_merge_heads(groups[:n], None, n), axis=0)


def _peer_kernel(x1_ref, nffn_ref, wqt_ref, keys_ref, u_ref, vt_ref, nfin_ref, y_out,
                 xn_s, qt_s, e1_s, l1_s, e2_s, r2_s, acc_s, *slabs, te):
    tm = x1_ref.shape[0]
    ng = tm // LANE
    nb = te // LANE
    j = pl.program_id(1)

    @pl.when(j == 0)
    def _route():
        xn = (_rms(x1_ref[...]) * nffn_ref[...]).astype(BF16)
        xn_s[...] = xn
        qt_s[...] = _dot_nt(wqt_ref[...], xn).astype(BF16)
        acc_s[...] = jnp.zeros_like(acc_s)
        for h in range(PEER_HEADS):
            for p, e_s in ((0, e1_s), (1, l1_s)):
                r0 = (2 * h + p) * PEER_HALF
                s = _dot(keys_ref[p], qt_s[r0:r0 + PEER_HALF, :])
                e = jnp.exp(s - _col_max(s))
                for g in range(ng):
                    e_s[g, h] = e[:, g * LANE:(g + 1) * LANE]

        def route_one(g, h):
            e1 = e1_s[g, h]
            e2 = l1_s[g, h]
            v1 = _top_values(e1, PEER_TOPK)
            v2 = _top_values(e2, PEER_TOPK)
            half = PEER_TOPK // 2

            def products(w2):
                return ([v1[:half] * w2[b:b + 1] for b in range(PEER_TOPK)], v1[half:] * w2[0:1])

            cand = products(v2)
            th = _merge_heads(*cand, PEER_TOPK)[-1]
            sel = ([c >= th for c in cand[0]], cand[1] >= th)

            def selected(prod, fill, combine):
                out = jnp.where(sel[1], prod[1], fill)
                for s, c in zip(sel[0], prod[0]):
                    out = combine(out, jnp.where(s, c, fill))
                return out

            inv_z = 1.0 / jnp.sum(selected(cand, 0.0, jnp.add), axis=0, keepdims=True)
            e2_s[g, h] = (e2 * inv_z).astype(BF16)
            count_lo = jnp.where(sel[0][0], 1.0, 0.0)
            for s in sel[0][1:]:
                count_lo = count_lo + jnp.where(s, 1.0, 0.0)
            count_hi = jnp.where(sel[1], 1.0, 0.0)
            counts = jnp.concatenate([count_lo, count_hi], axis=0)
            l1 = jnp.zeros_like(e1)
            for a in range(PEER_TOPK):
                l1 = jnp.where(e1 == v1[a:a + 1], counts[a:a + 1], l1)
            l1_s[g, h] = l1
            r2 = jnp.full_like(e2, float(PEER_TOPK))
            for b in reversed(range(PEER_TOPK)):
                r2 = jnp.where(e2 >= v2[b:b + 1], float(b), r2)
            r2_s[g, h] = r2.astype(BF16)

        def route_group(i, _):
            for k in range(ROUTE_UNROLL):
                item = ROUTE_UNROLL * i + k
                route_one(item // PEER_HEADS, item % PEER_HEADS)
            return 0

        lax.fori_loop(0, ng * PEER_HEADS // ROUTE_UNROLL, route_group, 0)

    def hidden(c):
        slabs[2 * c][...] = _dot_nt(u_ref[c * 2 * LANE:(c + 1) * 2 * LANE, :], xn_s[...])

    def accumulate(c):
        acc_s[...] += _dot(vt_ref[0, :, c * 2 * LANE:(c + 1) * 2 * LANE], slabs[2 * c + 1][...])

    hidden(0)
    for c in range(nb // 2):
        ht_s, a_s = slabs[2 * c], slabs[2 * c + 1]
        if c + 1 < nb // 2:
            hidden(c + 1)
        for i in range(2):
            r = slice(i * LANE, (i + 1) * LANE)
            i1 = j * nb + 2 * c + i
            for g in range(ng):
                cols = slice(g * LANE, (g + 1) * LANE)
                w = jnp.zeros((N_KEYS, LANE), BF16)
                for h in range(PEER_HEADS):
                    e2 = e2_s[g, h]
                    e1_row = jnp.broadcast_to(e1_s[g, h, pl.ds(i1, 1), :], e2.shape).astype(BF16)
                    l1_row = jnp.broadcast_to(l1_s[g, h, pl.ds(i1, 1), :], e2.shape).astype(BF16)
                    z = e1_row * e2
                    w = w + jnp.where(r2_s[g, h] < l1_row, z, jnp.zeros_like(z))
                a_s[r, cols] = w * _gelu(ht_s[r, cols]).astype(BF16)
        if c >= 1:
            accumulate(c - 1)
    accumulate(nb // 2 - 1)

    @pl.when(j == pl.num_programs(1) - 1)
    def _finish():
        x2 = x1_ref[...] + acc_s[...].T
        y_out[...] = _rms(x2) * nfin_ref[...]


def _peer(x1, w, tm):
    T, D = x1.shape
    NE = w["u"].shape[0]
    te = w["vt"].shape[2]
    assert T % tm == 0 and NE % te == 0 and tm % LANE == 0 and te % (2 * LANE) == 0
    ng = tm // LANE
    consts = [w["nffn"], w["wqt"], w["keys"]]
    return pl.pallas_call(
        functools.partial(_peer_kernel, te=te),
        grid=(T // tm, NE // te),
        in_specs=[pl.BlockSpec((tm, D), lambda i, j: (i, 0))]
        + [_const_spec(c.shape) for c in consts]
        + [pl.BlockSpec((te, D), lambda i, j: (j, 0)),
           pl.BlockSpec((1, D, te), lambda i, j: (j, 0, 0)),
           _const_spec(w["nfin"].shape)],
        out_specs=pl.BlockSpec((tm, D), lambda i, j: (i, 0)),
        out_shape=jax.ShapeDtypeStruct((T, D), F32),
        scratch_shapes=[
            pltpu.VMEM((tm, D), BF16),
            pltpu.VMEM((PEER_HEADS * 2 * PEER_HALF, tm), BF16),
            pltpu.VMEM((ng, PEER_HEADS, N_KEYS, LANE), F32),
            pltpu.VMEM((ng, PEER_HEADS, N_KEYS, LANE), F32),
            pltpu.VMEM((ng, PEER_HEADS, N_KEYS, LANE), BF16),
            pltpu.VMEM((ng, PEER_HEADS, N_KEYS, LANE), BF16),
            pltpu.VMEM((D, tm), F32),
        ] + [pltpu.VMEM((2 * LANE, tm), F32),
             pltpu.VMEM((2 * LANE, tm), BF16)] * (te // (2 * LANE)),
        compiler_params=pltpu.CompilerParams(
            dimension_semantics=("parallel", "arbitrary"), vmem_limit_bytes=VMEM_LIMIT),
        name="peer",
    )(x1, *consts, w["u"], w["vt"], w["nfin"])


def _rope_tables(seq):
    inv = 1.0 / (ROPE_THETA ** (jnp.arange(0, QK_ROPE, 2, dtype=F32) / QK_ROPE))
    ang = jnp.arange(seq, dtype=F32)[:, None] * inv[None, :]
    cos, sin = jnp.cos(ang), jnp.sin(ang)
    pad = jnp.zeros((seq, LANE - QK_ROPE), F32)
    return (jnp.concatenate([cos, cos, pad], axis=1), jnp.concatenate([sin, sin, pad], axis=1))


def _rot_cols(wpe):
    half = QK_ROPE // 2
    return jnp.concatenate([-wpe[..., half:], wpe[..., :half]], axis=-1)


def _prep_weights(norm_mix, w_in, q_norm, kv_norm, w_uq, w_ukv, w_o_b, sgu_norm, sgu_w, sgu_b,
                  w_o_a, w_out, norm_ffn, peer_wq, peer_keys, peer_u, peer_v, norm_final):
    a, b, c = Q_LORA, Q_LORA + KV_LORA, Q_LORA + KV_LORA + QK_ROPE
    d, e, f = c + SGU_WIDTH, c + 2 * SGU_WIDTH, c + 2 * SGU_WIDTH + D_MODEL
    zpad = lambda rows, n: jnp.zeros((rows, n), F32)
    w_kr = w_in[:, b:c]
    wkr = jnp.concatenate([w_kr, zpad(D_MODEL, LANE - QK_ROPE),
                           _rot_cols(w_kr), zpad(D_MODEL, LANE - QK_ROPE)], axis=1)
    uq = w_uq.reshape(Q_LORA, MLA_HEADS, QK_NOPE + QK_ROPE)
    hpad = jnp.zeros((Q_LORA, MLA_HEADS, LANE - QK_ROPE), F32)
    wuq = jnp.concatenate([uq, hpad], axis=-1).reshape(Q_LORA, MLA_HEADS * QK_PAD)
    wuqr = jnp.concatenate([_rot_cols(uq[..., QK_NOPE:]), hpad], axis=-1).reshape(Q_LORA, MLA_HEADS * LANE)
    ukv = w_ukv.reshape(KV_LORA, MLA_HEADS, QK_NOPE + V_HEAD)
    scale = (QK_NOPE + QK_ROPE) ** -0.5 * LOG2E
    sbias = jnp.repeat(sgu_b.T, SGU_WIDTH // SGU_GROUPS, axis=1)
    row = lambda v: v.reshape(1, -1).astype(F32)
    return {
        "nmix": row(norm_mix), "wq": w_in[:, :a].astype(BF16), "wkv": w_in[:, a:b].astype(BF16),
        "wkr": wkr.astype(BF16), "wu": w_in[:, c:d].astype(BF16), "wv": w_in[:, d:e].astype(BF16),
        "wga": w_in[:, e:f].astype(BF16), "wgb": w_in[:, f:].astype(BF16),
        "qn": row(q_norm) * scale, "kvn": row(kv_norm),
        "wuq": wuq.astype(BF16), "wuqr": wuqr.astype(BF16),
        "wuk": ukv[..., :QK_NOPE].reshape(KV_LORA, MLA_HEADS * QK_NOPE).astype(BF16),
        "wuvt": ukv[..., QK_NOPE:].reshape(KV_LORA, MLA_HEADS * V_HEAD).T.astype(BF16),
        "sgn": row(sgu_norm), "sw": sgu_w.astype(BF16), "sbias": sbias.astype(F32),
        "woa": w_o_a.astype(BF16), "wob": w_o_b.astype(BF16), "wout": w_out.astype(BF16),
        "nffn": row(norm_ffn), "wqt": peer_wq.T.astype(BF16), "keys": peer_keys.astype(BF16),
        "u": peer_u.astype(BF16), "nfin": row(norm_final),
        "vt": peer_v.reshape(-1, PEER_TE, D_MODEL).transpose(0, 2, 1).astype(BF16),
    }


def _trunk(x, w, front_tm=FRONT_TM, tq=ATTN_TQ, tk=ATTN_TK, post_tm=POST_TM,
           peer_tm=PEER_TM):
    B, S, D = x.shape
    cos, sin = _rope_tables(S)
    q, k, vt, ya, sgb = _front(x, cos, sin, w, front_tm)
    o = _attention(q, k, vt, tq, tk)
    flat = lambda t: t.reshape(B * S, t.shape[-1])
    x1 = _post(flat(x), flat(o), flat(ya), flat(sgb), w["wob"], w["wout"], post_tm)
    y = _peer(x1, w, peer_tm)
    return y.reshape(B, S, D)


def kernel(x_prompt, x_sample, norm_mix, w_in, q_norm, kv_norm, w_uq, w_ukv, w_o_b, sgu_norm,
           sgu_w, sgu_b, w_o_a, w_out, norm_ffn, peer_wq, peer_keys, peer_u, peer_v, norm_final):
    assert norm_mix.shape[0] == 1, "single layer"
    w = _prep_weights(norm_mix[0], w_in[0], q_norm[0], kv_norm[0], w_uq[0], w_ukv[0], w_o_b[0],
                      sgu_norm[0], sgu_w[0], sgu_b[0], w_o_a[0], w_out[0], norm_ffn[0],
                      peer_wq[0], peer_keys[0], peer_u[0], peer_v[0], norm_final)
    return (_trunk(x_prompt, w), _trunk(x_sample, w))
```

```python
import functools

import jax
import jax.numpy as jnp
from jax import lax
from jax.experimental import pallas as pl
from jax.experimental.pallas import tpu as pltpu

F32 = jnp.float32
BF16 = jnp.bfloat16

D_MODEL = 1024
SGU_CHUNK = 128
SGU_GROUPS = 8
SGU_WIDTH = 1024
MLA_HEADS = 8
QK_NOPE = 128
QK_ROPE = 64
V_HEAD = 128
Q_LORA = 384
KV_LORA = 256
ROPE_THETA = 10000.0
PEER_HEADS = 8
N_KEYS = 128
PEER_HALF = 128
PEER_TOPK = 16
EPS = 1e-6

LANE = 128
SUBLANE = 8
QK_PAD = 256
LOG2E = 1.4426950408889634
VMEM_LIMIT = 56 * 1024 * 1024

FRONT_TM = 512
ATTN_TQ = 1024
ATTN_TK = 1024
ATTN_UNROLL = 4
POST_TM = 1024
PEER_TM = 256
PEER_TE = 4096
ROUTE_UNROLL = 8


def _rms(x, eps=EPS):
    return x * lax.rsqrt(jnp.mean(x * x, axis=-1, keepdims=True) + eps)


_GELU_K0 = -2.0 * 0.7978845608028654 * LOG2E
_GELU_K1 = _GELU_K0 * 0.044715


def _gelu(x):
    return x / (1.0 + jnp.exp2(x * (_GELU_K0 + _GELU_K1 * (x * x))))


def _dot(a, b):
    return jnp.dot(a, b, preferred_element_type=F32)


def _dot_nt(a, b):
    return lax.dot_general(a, b, (((1,), (1,)), ((), ())), preferred_element_type=F32)


def _const_spec(shape):
    nd = len(shape)
    return pl.BlockSpec(shape, lambda *_: (0,) * nd, pipeline_mode=pl.Buffered(1))


def _front_kernel(x_ref, cos_ref, sin_ref, nmix_ref, wq_ref, wkv_ref, wkr_ref, wu_ref, wv_ref,
                  wga_ref, wgb_ref, qn_ref, kvn_ref, wuq_ref, wuqr_ref, wuk_ref, wuvt_ref,
                  sgn_ref, sw_ref, sbias_ref, woa_ref,
                  q_out, k_out, vt_out, ya_out, sgb_out, vn_s, sgu_s):
    tm = x_ref.shape[1]
    x = x_ref[0]
    xn = (_rms(x) * nmix_ref[...]).astype(BF16)
    cos = cos_ref[...]
    sin = sin_ref[...]

    c_q = (_rms(_dot(xn, wq_ref[...])) * qn_ref[...]).astype(BF16)
    qa = _dot(c_q, wuq_ref[...])
    qr = _dot(c_q, wuqr_ref[...])
    for h in range(MLA_HEADS):
        q_out[0, :, h * QK_PAD:h * QK_PAD + LANE] = qa[:, h * QK_PAD:h * QK_PAD + LANE].astype(BF16)
        pe = qa[:, h * QK_PAD + LANE:(h + 1) * QK_PAD] * cos + qr[:, h * LANE:(h + 1) * LANE] * sin
        q_out[0, :, h * QK_PAD + LANE:(h + 1) * QK_PAD] = pe.astype(BF16)

    c_kv = (_rms(_dot(xn, wkv_ref[...])) * kvn_ref[...]).astype(BF16)
    kn = _dot(c_kv, wuk_ref[...])
    kr = _dot(xn, wkr_ref[...])
    kpe = (kr[:, :LANE] * cos + kr[:, LANE:] * sin).astype(BF16)
    for h in range(MLA_HEADS):
        k_out[0, :, h * QK_PAD:h * QK_PAD + LANE] = kn[:, h * LANE:(h + 1) * LANE].astype(BF16)
        k_out[0, :, h * QK_PAD + LANE:(h + 1) * QK_PAD] = kpe
    vt_out[0] = _dot_nt(wuvt_ref[...], c_kv).astype(BF16)

    v = _gelu(_dot(xn, wv_ref[...]))
    vn_s[...] = (_rms(v) * sgn_ref[...]).astype(BF16)
    u = _gelu(_dot(xn, wu_ref[...]))
    for c in range(tm // SGU_CHUNK):
        rows = slice(c * SGU_CHUNK, (c + 1) * SGU_CHUNK)
        for g in range(SGU_GROUPS):
            cols = slice(g * LANE, (g + 1) * LANE)
            mixed = _dot(sw_ref[g], vn_s[rows, cols]) + sbias_ref[:, cols]
            sgu_s[rows, cols] = (u[rows, cols] * mixed).astype(BF16)
    ya = _dot(sgu_s[...], woa_ref[...])
    ya_out[0] = (jax.nn.sigmoid(_dot(xn, wga_ref[...])) * ya).astype(BF16)
    sgb_out[0] = jax.nn.sigmoid(_dot(xn, wgb_ref[...])).astype(BF16)


def _front(x, cos, sin, w, tm):
    B, S, D = x.shape
    assert S % tm == 0 and tm % SGU_CHUNK == 0
    nt = S // tm
    tok = lambda width: pl.BlockSpec((1, tm, width), lambda b, i: (b, i, 0))
    pos = pl.BlockSpec((tm, LANE), lambda b, i: (i, 0))
    consts = [w["nmix"], w["wq"], w["wkv"], w["wkr"], w["wu"], w["wv"], w["wga"], w["wgb"],
              w["qn"], w["kvn"], w["wuq"], w["wuqr"], w["wuk"], w["wuvt"], w["sgn"], w["sw"],
              w["sbias"], w["woa"]]
    out_shape = (
        jax.ShapeDtypeStruct((B, S, MLA_HEADS * QK_PAD), BF16),
        jax.ShapeDtypeStruct((B, S, MLA_HEADS * QK_PAD), BF16),
        jax.ShapeDtypeStruct((B, MLA_HEADS * V_HEAD, S), BF16),
        jax.ShapeDtypeStruct((B, S, D), BF16),
        jax.ShapeDtypeStruct((B, S, D), BF16),
    )
    return pl.pallas_call(
        _front_kernel,
        grid=(B, nt),
        in_specs=[tok(D), pos, pos] + [_const_spec(c.shape) for c in consts],
        out_specs=(tok(MLA_HEADS * QK_PAD), tok(MLA_HEADS * QK_PAD),
                   pl.BlockSpec((1, MLA_HEADS * V_HEAD, tm), lambda b, i: (b, 0, i)),
                   tok(D), tok(D)),
        out_shape=out_shape,
        scratch_shapes=[pltpu.VMEM((tm, SGU_WIDTH), BF16), pltpu.VMEM((tm, SGU_WIDTH), BF16)],
        compiler_params=pltpu.CompilerParams(
            dimension_semantics=("parallel", "parallel"), vmem_limit_bytes=VMEM_LIMIT),
        name="front",
    )(x, cos, sin, *consts)


def _attn_kernel(q_ref, k_ref, vt_ref, o_ref, sa_s, sb_s, *, tk):
    tq = q_ref.shape[1]
    nk = k_ref.shape[1] // tk
    q = q_ref[0]

    def scores(s_ref, c):
        start = pl.multiple_of(c * tk, tk)
        s = _dot_nt(k_ref[0, pl.ds(start, tk), :], q)
        s_ref[...] = s
        return jnp.max(s, axis=0, keepdims=True)

    def update(s_ref, mc, c, carry):
        m, l, acc = carry
        start = pl.multiple_of(c * tk, tk)
        m_new = jnp.maximum(m, mc)
        alpha = jnp.exp2(m - m_new)
        p = jnp.exp2(s_ref[...] - m_new)
        l = alpha * l + jnp.sum(p, axis=0, keepdims=True)
        acc = alpha * acc + _dot(vt_ref[0, :, pl.ds(start, tk)], p.astype(BF16))
        return m_new, l, acc

    def run(c0, n, mc, state, more):
        bufs = (sa_s, sb_s)
        for t in range(n):
            mc_next = scores(bufs[(t + 1) % 2], c0 + t + 1) if (t + 1 < n or more) else None
            state = update(bufs[t % 2], mc, c0 + t, state)
            mc = mc_next
        return mc, state

    unroll = ATTN_UNROLL if nk >= 4 * ATTN_UNROLL else 2

    def body(i, carry):
        return run(i * unroll, unroll, *carry, more=True)

    state = (jnp.full((1, tq), -jnp.inf, F32), jnp.zeros((1, tq), F32), jnp.zeros((V_HEAD, tq), F32))
    nloop = nk // unroll - 1
    mc, state = lax.fori_loop(0, nloop, body, (scores(sa_s, 0), state))
    _, (_, l, acc) = run(nloop * unroll, unroll, mc, state, more=False)
    o_ref[0] = (acc / l).T.astype(o_ref.dtype)


def _attention(q, k, vt, tq, tk):
    B, S, _ = q.shape
    assert S % tq == 0 and S % (ATTN_UNROLL * tk) == 0 and ATTN_UNROLL % 4 == 0
    return pl.pallas_call(
        functools.partial(_attn_kernel, tk=tk),
        scratch_shapes=[pltpu.VMEM((tk, tq), F32)] * 2,
        grid=(B, MLA_HEADS, S // tq),
        in_specs=[
            pl.BlockSpec((1, tq, QK_PAD), lambda b, h, i: (b, i, h)),
            pl.BlockSpec((1, S, QK_PAD), lambda b, h, i: (b, 0, h)),
            pl.BlockSpec((1, V_HEAD, S), lambda b, h, i: (b, h, 0)),
        ],
        out_specs=pl.BlockSpec((1, tq, V_HEAD), lambda b, h, i: (b, i, h)),
        out_shape=jax.ShapeDtypeStruct((B, S, MLA_HEADS * V_HEAD), BF16),
        compiler_params=pltpu.CompilerParams(
            dimension_semantics=("parallel", "parallel", "arbitrary"),
            vmem_limit_bytes=VMEM_LIMIT),
        name="attn",
    )(q, k, vt)


def _post_kernel(x_ref, o_ref, ya_ref, sgb_ref, wob_ref, wout_ref, x1_out):
    yb = _dot(o_ref[...], wob_ref[...])
    merged = ya_ref[...].astype(F32) + sgb_ref[...].astype(F32) * yb
    x1_out[...] = x_ref[...] + _dot(merged.astype(BF16), wout_ref[...])


def _post(x2, o2, ya2, sgb2, wob, wout, tm):
    T, D = x2.shape
    assert T % tm == 0
    tok = pl.BlockSpec((tm, D), lambda i: (i, 0))
    return pl.pallas_call(
        _post_kernel,
        grid=(T // tm,),
        in_specs=[tok, tok, tok, tok, _const_spec(wob.shape), _const_spec(wout.shape)],
        out_specs=tok,
        out_shape=jax.ShapeDtypeStruct((T, D), F32),
        compiler_params=pltpu.CompilerParams(
            dimension_semantics=("parallel",), vmem_limit_bytes=VMEM_LIMIT),
        name="post",
    )(x2, o2, ya2, sgb2, wob, wout)


def _col_max(a):
    return jnp.max(a, axis=0, keepdims=True)


def _sorting_network(n):
    comps = []
    p = 1
    while p < n:
        k = p
        while k >= 1:
            for j in range(k % p, n - k, 2 * k):
                for i in range(min(k, n - j - k)):
                    if (i + j) // (2 * p) == (i + j + k) // (2 * p):
                        comps.append((i + j, i + j + k))
            k //= 2
        p *= 2
    return comps


def _merge_heads(lists, extra, n):
    lists = list(lists)
    vals = []
    for t in range(n):
        head = lists[0] if extra is None else jnp.maximum(lists[0], extra)
        m = jnp.maximum(_col_max(head), 0.0)
        vals.append(m)
        if t == n - 1:
            break
        hit = lists[0] == m
        live = min(len(lists) - 1, n - 1 - t)
        for d in range(live):
            lists[d] = jnp.where(hit, lists[d + 1], lists[d])
        if live == len(lists) - 1:
            lists[live] = jnp.where(hit, -1.0, lists[live])
        if extra is not None:
            extra = jnp.where(extra == m, -1.0, extra)
    return vals


def _top_values(e, n):
    groups = [e[SUBLANE * r:SUBLANE * (r + 1)] for r in range(e.shape[0] // SUBLANE)]
    for a, b in _sorting_network(len(groups)):
        groups[a], groups[b] = jnp.maximum(groups[a], groups[b]), jnp.minimum(groups[a], groups[b])
    return jnp.concatenate(_merge_heads(groups[:n], None, n), axis=0)


def _peer_kernel(x1_ref, nffn_ref, wqt_ref, keys_ref, u_ref, vt_ref, nfin_ref, y_out,
                 xn_s, qt_s, e1_s, e2_s, th_s, acc_s, *slabs, te):
    tm = x1_ref.shape[0]
    ng = tm // LANE
    nb = te // LANE
    j = pl.program_id(1)

    @pl.when(j == 0)
    def _route():
        xn = (_rms(x1_ref[...]) * nffn_ref[...]).astype(BF16)
        xn_s[...] = xn
        qt_s[...] = _dot_nt(wqt_ref[...], xn).astype(BF16)
        acc_s[...] = jnp.zeros_like(acc_s)
        for h in range(PEER_HEADS):
            for p, e_s in ((0, e1_s), (1, e2_s)):
                r0 = (2 * h + p) * PEER_HALF
                s = _dot(keys_ref[p], qt_s[r0:r0 + PEER_HALF, :])
                e = jnp.exp(s - _col_max(s))
                for g in range(ng):
                    e_s[g, h] = e[:, g * LANE:(g + 1) * LANE]

        def route_one(g, h):
            e1 = e1_s[g, h]
            e2 = e2_s[g, h]
            v1 = _top_values(e1, PEER_TOPK)
            v2 = _top_values(e2, PEER_TOPK)
            half = PEER_TOPK // 2

            def products(w2):
                return ([v1[:half] * w2[b:b + 1] for b in range(PEER_TOPK)], v1[half:] * w2[0:1])

            cand = products(v2)
            th = _merge_heads(*cand, PEER_TOPK)[-1]
            sel = ([c >= th for c in cand[0]], cand[1] >= th)

            def selected(prod, fill, combine):
                out = jnp.where(sel[1], prod[1], fill)
                for s, c in zip(sel[0], prod[0]):
                    out = combine(out, jnp.where(s, c, fill))
                return out

            inv_z = 1.0 / jnp.sum(selected(cand, 0.0, jnp.add), axis=0, keepdims=True)
            e2_s[g, h] = e2 * inv_z
            thn = jnp.min(selected(products(v2 * inv_z), jnp.inf, jnp.minimum), axis=0, keepdims=True)
            th_s[g, h] = jnp.broadcast_to(thn, (SUBLANE, LANE))

        def route_group(i, _):
            for k in range(ROUTE_UNROLL):
                item = ROUTE_UNROLL * i + k
                route_one(item // PEER_HEADS, item % PEER_HEADS)
            return 0

        lax.fori_loop(0, ng * PEER_HEADS // ROUTE_UNROLL, route_group, 0)

    def hidden(c):
        slabs[2 * c][...] = _dot_nt(u_ref[c * 2 * LANE:(c + 1) * 2 * LANE, :], xn_s[...])

    def accumulate(c):
        acc_s[...] += _dot(vt_ref[0, :, c * 2 * LANE:(c + 1) * 2 * LANE], slabs[2 * c + 1][...])

    hidden(0)
    for c in range(nb // 2):
        ht_s, a_s = slabs[2 * c], slabs[2 * c + 1]
        if c + 1 < nb // 2:
            hidden(c + 1)
        for i in range(2):
            r = slice(i * LANE, (i + 1) * LANE)
            i1 = j * nb + 2 * c + i
            for g in range(ng):
                cols = slice(g * LANE, (g + 1) * LANE)
                w = jnp.zeros((LANE, LANE), F32)
                for h in range(PEER_HEADS):
                    z = e1_s[g, h, pl.ds(i1, 1), :] * e2_s[g, h]
                    w = w + jnp.where(z >= th_s[g, h, 0:1, :], z, 0.0)
                a_s[r, cols] = (w * _gelu(ht_s[r, cols])).astype(BF16)
        if c >= 1:
            accumulate(c - 1)
    accumulate(nb // 2 - 1)

    @pl.when(j == pl.num_programs(1) - 1)
    def _finish():
        x2 = x1_ref[...] + acc_s[...].T
        y_out[...] = _rms(x2) * nfin_ref[...]


def _peer(x1, w, tm):
    T, D = x1.shape
    NE = w["u"].shape[0]
    te = w["vt"].shape[2]
    assert T % tm == 0 and NE % te == 0 and tm % LANE == 0 and te % (2 * LANE) == 0
    ng = tm // LANE
    consts = [w["nffn"], w["wqt"], w["keys"]]
    return pl.pallas_call(
        functools.partial(_peer_kernel, te=te),
        grid=(T // tm, NE // te),
        in_specs=[pl.BlockSpec((tm, D), lambda i, j: (i, 0))]
        + [_const_spec(c.shape) for c in consts]
        + [pl.BlockSpec((te, D), lambda i, j: (j, 0)),
           pl.BlockSpec((1, D, te), lambda i, j: (j, 0, 0)),
           _const_spec(w["nfin"].shape)],
        out_specs=pl.BlockSpec((tm, D), lambda i, j: (i, 0)),
        out_shape=jax.ShapeDtypeStruct((T, D), F32),
        scratch_shapes=[
            pltpu.VMEM((tm, D), BF16),
            pltpu.VMEM((PEER_HEADS * 2 * PEER_HALF, tm), BF16),
            pltpu.VMEM((ng, PEER_HEADS, N_KEYS, LANE), F32),
            pltpu.VMEM((ng, PEER_HEADS, N_KEYS, LANE), F32),
            pltpu.VMEM((ng, PEER_HEADS, SUBLANE, LANE), F32),
            pltpu.VMEM((D, tm), F32),
        ] + [pltpu.VMEM((2 * LANE, tm), F32),
             pltpu.VMEM((2 * LANE, tm), BF16)] * (te // (2 * LANE)),
        compiler_params=pltpu.CompilerParams(
            dimension_semantics=("parallel", "arbitrary"), vmem_limit_bytes=VMEM_LIMIT),
        name="peer",
    )(x1, *consts, w["u"], w["vt"], w["nfin"])


def _rope_tables(seq):
    inv = 1.0 / (ROPE_THETA ** (jnp.arange(0, QK_ROPE, 2, dtype=F32) / QK_ROPE))
    ang = jnp.arange(seq, dtype=F32)[:, None] * inv[None, :]
    cos, sin = jnp.cos(ang), jnp.sin(ang)
    pad = jnp.zeros((seq, LANE - QK_ROPE), F32)
    return (jnp.concatenate([cos, cos, pad], axis=1), jnp.concatenate([sin, sin, pad], axis=1))


def _rot_cols(wpe):
    half = QK_ROPE // 2
    return jnp.concatenate([-wpe[..., half:], wpe[..., :half]], axis=-1)


def _prep_weights(norm_mix, w_in, q_norm, kv_norm, w_uq, w_ukv, w_o_b, sgu_norm, sgu_w, sgu_b,
                  w_o_a, w_out, norm_ffn, peer_wq, peer_keys, peer_u, peer_v, norm_final):
    a, b, c = Q_LORA, Q_LORA + KV_LORA, Q_LORA + KV_LORA + QK_ROPE
    d, e, f = c + SGU_WIDTH, c + 2 * SGU_WIDTH, c + 2 * SGU_WIDTH + D_MODEL
    zpad = lambda rows, n: jnp.zeros((rows, n), F32)
    w_kr = w_in[:, b:c]
    wkr = jnp.concatenate([w_kr, zpad(D_MODEL, LANE - QK_ROPE),
                           _rot_cols(w_kr), zpad(D_MODEL, LANE - QK_ROPE)], axis=1)
    uq = w_uq.reshape(Q_LORA, MLA_HEADS, QK_NOPE + QK_ROPE)
    hpad = jnp.zeros((Q_LORA, MLA_HEADS, LANE - QK_ROPE), F32)
    wuq = jnp.concatenate([uq, hpad], axis=-1).reshape(Q_LORA, MLA_HEADS * QK_PAD)
    wuqr = jnp.concatenate([_rot_cols(uq[..., QK_NOPE:]), hpad], axis=-1).reshape(Q_LORA, MLA_HEADS * LANE)
    ukv = w_ukv.reshape(KV_LORA, MLA_HEADS, QK_NOPE + V_HEAD)
    scale = (QK_NOPE + QK_ROPE) ** -0.5 * LOG2E
    sbias = jnp.repeat(sgu_b.T, SGU_WIDTH // SGU_GROUPS, axis=1)
    row = lambda v: v.reshape(1, -1).astype(F32)
    return {
        "nmix": row(norm_mix), "wq": w_in[:, :a].astype(BF16), "wkv": w_in[:, a:b].astype(BF16),
        "wkr": wkr.astype(BF16), "wu": w_in[:, c:d].astype(BF16), "wv": w_in[:, d:e].astype(BF16),
        "wga": w_in[:, e:f].astype(BF16), "wgb": w_in[:, f:].astype(BF16),
        "qn": row(q_norm) * scale, "kvn": row(kv_norm),
        "wuq": wuq.astype(BF16), "wuqr": wuqr.astype(BF16),
        "wuk": ukv[..., :QK_NOPE].reshape(KV_LORA, MLA_HEADS * QK_NOPE).astype(BF16),
        "wuvt": ukv[..., QK_NOPE:].reshape(KV_LORA, MLA_HEADS * V_HEAD).T.astype(BF16),
        "sgn": row(sgu_norm), "sw": sgu_w.astype(BF16), "sbias": sbias.astype(F32),
        "woa": w_o_a.astype(BF16), "wob": w_o_b.astype(BF16), "wout": w_out.astype(BF16),
        "nffn": row(norm_ffn), "wqt": peer_wq.T.astype(BF16), "keys": peer_keys.astype(BF16),
        "u": peer_u.astype(BF16), "nfin": row(norm_final),
        "vt": peer_v.reshape(-1, PEER_TE, D_MODEL).transpose(0, 2, 1).astype(BF16),
    }


def _trunk(x, w, front_tm=FRONT_TM, tq=ATTN_TQ, tk=ATTN_TK, post_tm=POST_TM,
           peer_tm=PEER_TM):
    B, S, D = x.shape
    cos, sin = _rope_tables(S)
    q, k, vt, ya, sgb = _front(x, cos, sin, w, front_tm)
    o = _attention(q, k, vt, tq, tk)
    flat = lambda t: t.reshape(B * S, t.shape[-1])
    x1 = _post(flat(x), flat(o), flat(ya), flat(sgb), w["wob"], w["wout"], post_tm)
    y = _peer(x1, w, peer_tm)
    return y.reshape(B, S, D)


def kernel(x_prompt, x_sample, norm_mix, w_in, q_norm, kv_norm, w_uq, w_ukv, w_o_b, sgu_norm,
           sgu_w, sgu_b, w_o_a, w_out, norm_ffn, peer_wq, peer_keys, peer_u, peer_v, norm_final):
    assert norm_mix.shape[0] == 1, "single layer"
    w = _prep_weights(norm_mix[0], w_in[0], q_norm[0], kv_norm[0], w_uq[0], w_ukv[0], w_o_b[0],
                      sgu_norm[0], sgu_w[0], sgu_b[0], w_o_a[0], w_out[0], norm_ffn[0],
                      peer_wq[0], peer_keys[0], peer_u[0], peer_v[0], norm_final)
    return (_trunk(x_prompt, w), _trunk(x_sample, w))
```

```python
import functools

import jax
import jax.numpy as jnp
from jax import lax
from jax.experimental import pallas as pl
from jax.experimental.pallas import tpu as pltpu

F32 = jnp.float32
BF16 = jnp.bfloat16

D_MODEL = 1024
SGU_CHUNK = 128
SGU_GROUPS = 8
SGU_WIDTH = 1024
MLA_HEADS = 8
QK_NOPE = 128
QK_ROPE = 64
V_HEAD = 128
Q_LORA = 384
KV_LORA = 256
ROPE_THETA = 10000.0
PEER_HEADS = 8
N_KEYS = 128
PEER_HALF = 128
PEER_TOPK = 16
EPS = 1e-6

LANE = 128
SUBLANE = 8
QK_PAD = 256
LOG2E = 1.4426950408889634
VMEM_LIMIT = 56 * 1024 * 1024

FRONT_TM = 512
ATTN_TQ = 1024
ATTN_TK = 1024
ATTN_UNROLL = 4
POST_TM = 1024
PEER_TM = 512
PEER_TE = 2048
ROUTE_UNROLL = 8


def _rms(x, eps=EPS):
    return x * lax.rsqrt(jnp.mean(x * x, axis=-1, keepdims=True) + eps)


_GELU_K0 = -2.0 * 0.7978845608028654 * LOG2E
_GELU_K1 = _GELU_K0 * 0.044715


def _gelu(x):
    return x / (1.0 + jnp.exp2(x * (_GELU_K0 + _GELU_K1 * (x * x))))


def _dot(a, b):
    return jnp.dot(a, b, preferred_element_type=F32)


def _dot_nt(a, b):
    return lax.dot_general(a, b, (((1,), (1,)), ((), ())), preferred_element_type=F32)


def _const_spec(shape):
    nd = len(shape)
    return pl.BlockSpec(shape, lambda *_: (0,) * nd, pipeline_mode=pl.Buffered(1))


def _front_kernel(x_ref, cos_ref, sin_ref, nmix_ref, wq_ref, wkv_ref, wkr_ref, wu_ref, wv_ref,
                  wga_ref, wgb_ref, qn_ref, kvn_ref, wuq_ref, wuqr_ref, wuk_ref, wuvt_ref,
                  sgn_ref, sw_ref, sbias_ref, woa_ref,
                  q_out, k_out, vt_out, ya_out, sgb_out, vn_s, sgu_s):
    tm = x_ref.shape[1]
    x = x_ref[0]
    xn = (_rms(x) * nmix_ref[...]).astype(BF16)
    cos = cos_ref[...]
    sin = sin_ref[...]

    c_q = (_rms(_dot(xn, wq_ref[...])) * qn_ref[...]).astype(BF16)
    qa = _dot(c_q, wuq_ref[...])
    qr = _dot(c_q, wuqr_ref[...])
    for h in range(MLA_HEADS):
        q_out[0, :, h * QK_PAD:h * QK_PAD + LANE] = qa[:, h * QK_PAD:h * QK_PAD + LANE].astype(BF16)
        pe = qa[:, h * QK_PAD + LANE:(h + 1) * QK_PAD] * cos + qr[:, h * LANE:(h + 1) * LANE] * sin
        q_out[0, :, h * QK_PAD + LANE:(h + 1) * QK_PAD] = pe.astype(BF16)

    c_kv = (_rms(_dot(xn, wkv_ref[...])) * kvn_ref[...]).astype(BF16)
    kn = _dot(c_kv, wuk_ref[...])
    kr = _dot(xn, wkr_ref[...])
    kpe = (kr[:, :LANE] * cos + kr[:, LANE:] * sin).astype(BF16)
    for h in range(MLA_HEADS):
        k_out[0, :, h * QK_PAD:h * QK_PAD + LANE] = kn[:, h * LANE:(h + 1) * LANE].astype(BF16)
        k_out[0, :, h * QK_PAD + LANE:(h + 1) * QK_PAD] = kpe
    vt_out[0] = _dot_nt(wuvt_ref[...], c_kv).astype(BF16)

    v = _gelu(_dot(xn, wv_ref[...]))
    vn_s[...] = (_rms(v) * sgn_ref[...]).astype(BF16)
    u = _gelu(_dot(xn, wu_ref[...]))
    for c in range(tm // SGU_CHUNK):
        rows = slice(c * SGU_CHUNK, (c + 1) * SGU_CHUNK)
        for g in range(SGU_GROUPS):
            cols = slice(g * LANE, (g + 1) * LANE)
            mixed = _dot(sw_ref[g], vn_s[rows, cols]) + sbias_ref[:, cols]
            sgu_s[rows, cols] = (u[rows, cols] * mixed).astype(BF16)
    ya = _dot(sgu_s[...], woa_ref[...])
    ya_out[0] = (jax.nn.sigmoid(_dot(xn, wga_ref[...])) * ya).astype(BF16)
    sgb_out[0] = jax.nn.sigmoid(_dot(xn, wgb_ref[...])).astype(BF16)


def _front(x, cos, sin, w, tm):
    B, S, D = x.shape
    assert S % tm == 0 and tm % SGU_CHUNK == 0
    nt = S // tm
    tok = lambda width: pl.BlockSpec((1, tm, width), lambda b, i: (b, i, 0))
    pos = pl.BlockSpec((tm, LANE), lambda b, i: (i, 0))
    consts = [w["nmix"], w["wq"], w["wkv"], w["wkr"], w["wu"], w["wv"], w["wga"], w["wgb"],
              w["qn"], w["kvn"], w["wuq"], w["wuqr"], w["wuk"], w["wuvt"], w["sgn"], w["sw"],
              w["sbias"], w["woa"]]
    out_shape = (
        jax.ShapeDtypeStruct((B, S, MLA_HEADS * QK_PAD), BF16),
        jax.ShapeDtypeStruct((B, S, MLA_HEADS * QK_PAD), BF16),
        jax.ShapeDtypeStruct((B, MLA_HEADS * V_HEAD, S), BF16),
        jax.ShapeDtypeStruct((B, S, D), BF16),
        jax.ShapeDtypeStruct((B, S, D), BF16),
    )
    return pl.pallas_call(
        _front_kernel,
        grid=(B, nt),
        in_specs=[tok(D), pos, pos] + [_const_spec(c.shape) for c in consts],
        out_specs=(tok(MLA_HEADS * QK_PAD), tok(MLA_HEADS * QK_PAD),
                   pl.BlockSpec((1, MLA_HEADS * V_HEAD, tm), lambda b, i: (b, 0, i)),
                   tok(D), tok(D)),
        out_shape=out_shape,
        scratch_shapes=[pltpu.VMEM((tm, SGU_WIDTH), BF16), pltpu.VMEM((tm, SGU_WIDTH), BF16)],
        compiler_params=pltpu.CompilerParams(
            dimension_semantics=("parallel", "parallel"), vmem_limit_bytes=VMEM_LIMIT),
        name="front",
    )(x, cos, sin, *consts)


def _attn_kernel(q_ref, k_ref, vt_ref, o_ref, sa_s, sb_s, *, tk):
    tq = q_ref.shape[1]
    nk = k_ref.shape[1] // tk
    q = q_ref[0]

    def scores(s_ref, c):
        start = pl.multiple_of(c * tk, tk)
        s = _dot_nt(k_ref[0, pl.ds(start, tk), :], q)
        s_ref[...] = s
        return jnp.max(s, axis=0, keepdims=True)

    def update(s_ref, mc, c, carry):
        m, l, acc = carry
        start = pl.multiple_of(c * tk, tk)
        m_new = jnp.maximum(m, mc)
        alpha = jnp.exp2(m - m_new)
        p = jnp.exp2(s_ref[...] - m_new)
        l = alpha * l + jnp.sum(p, axis=0, keepdims=True)
        acc = alpha * acc + _dot(vt_ref[0, :, pl.ds(start, tk)], p.astype(BF16))
        return m_new, l, acc

    def run(c0, n, mc, state, more):
        bufs = (sa_s, sb_s)
        for t in range(n):
            mc_next = scores(bufs[(t + 1) % 2], c0 + t + 1) if (t + 1 < n or more) else None
            state = update(bufs[t % 2], mc, c0 + t, state)
            mc = mc_next
        return mc, state

    unroll = ATTN_UNROLL if nk >= 4 * ATTN_UNROLL else 2

    def body(i, carry):
        return run(i * unroll, unroll, *carry, more=True)

    state = (jnp.full((1, tq), -jnp.inf, F32), jnp.zeros((1, tq), F32), jnp.zeros((V_HEAD, tq), F32))
    nloop = nk // unroll - 1
    mc, state = lax.fori_loop(0, nloop, body, (scores(sa_s, 0), state))
    _, (_, l, acc) = run(nloop * unroll, unroll, mc, state, more=False)
    o_ref[0] = (acc / l).T.astype(o_ref.dtype)


def _attention(q, k, vt, tq, tk):
    B, S, _ = q.shape
    assert S % tq == 0 and S % (ATTN_UNROLL * tk) == 0 and ATTN_UNROLL % 4 == 0
    return pl.pallas_call(
        functools.partial(_attn_kernel, tk=tk),
        scratch_shapes=[pltpu.VMEM((tk, tq), F32)] * 2,
        grid=(B, MLA_HEADS, S // tq),
        in_specs=[
            pl.BlockSpec((1, tq, QK_PAD), lambda b, h, i: (b, i, h)),
            pl.BlockSpec((1, S, QK_PAD), lambda b, h, i: (b, 0, h)),
            pl.BlockSpec((1, V_HEAD, S), lambda b, h, i: (b, h, 0)),
        ],
        out_specs=pl.BlockSpec((1, tq, V_HEAD), lambda b, h, i: (b, i, h)),
        out_shape=jax.ShapeDtypeStruct((B, S, MLA_HEADS * V_HEAD), BF16),
        compiler_params=pltpu.CompilerParams(
            dimension_semantics=("parallel", "parallel", "arbitrary"),
            vmem_limit_bytes=VMEM_LIMIT),
        name="attn",
    )(q, k, vt)


def _post_kernel(x_ref, o_ref, ya_ref, sgb_ref, wob_ref, wout_ref, x1_out):
    yb = _dot(o_ref[...], wob_ref[...])
    merged = ya_ref[...].astype(F32) + sgb_ref[...].astype(F32) * yb
    x1_out[...] = x_ref[...] + _dot(merged.astype(BF16), wout_ref[...])


def _post(x2, o2, ya2, sgb2, wob, wout, tm):
    T, D = x2.shape
    assert T % tm == 0
    tok = pl.BlockSpec((tm, D), lambda i: (i, 0))
    return pl.pallas_call(
        _post_kernel,
        grid=(T // tm,),
        in_specs=[tok, tok, tok, tok, _const_spec(wob.shape), _const_spec(wout.shape)],
        out_specs=tok,
        out_shape=jax.ShapeDtypeStruct((T, D), F32),
        compiler_params=pltpu.CompilerParams(
            dimension_semantics=("parallel",), vmem_limit_bytes=VMEM_LIMIT),
        name="post",
    )(x2, o2, ya2, sgb2, wob, wout)


def _col_max(a):
    return jnp.max(a, axis=0, keepdims=True)


def _sorting_network(n):
    comps = []
    p = 1
    while p < n:
        k = p
        while k >= 1:
            for j in range(k % p, n - k, 2 * k):
                for i in range(min(k, n - j - k)):
                    if (i + j) // (2 * p) == (i + j + k) // (2 * p):
                        comps.append((i + j, i + j + k))
            k //= 2
        p *= 2
    return comps


def _merge_heads(lists, extra, n):
    lists = list(lists)
    vals = []
    for t in range(n):
        head = lists[0] if extra is None else jnp.maximum(lists[0], extra)
        m = jnp.maximum(_col_max(head), 0.0)
        vals.append(m)
        if t == n - 1:
            break
        hit = lists[0] == m
        live = min(len(lists) - 1, n - 1 - t)
        for d in range(live):
            lists[d] = jnp.where(hit, lists[d + 1], lists[d])
        if live == len(lists) - 1:
            lists[live] = jnp.where(hit, -1.0, lists[live])
        if extra is not None:
            extra = jnp.where(extra == m, -1.0, extra)
    return vals


def _top_values(e, n):
    groups = [e[SUBLANE * r:SUBLANE * (r + 1)] for r in range(e.shape[0] // SUBLANE)]
    for a, b in _sorting_network(len(groups)):
        groups[a], groups[b] = jnp.maximum(groups[a], groups[b]), jnp.minimum(groups[a], groups[b])
    return jnp.concatenate(_merge_heads(groups[:n], None, n), axis=0)


def _peer_kernel(x1_ref, nffn_ref, wqt_ref, keys_ref, u_ref, vt_ref, nfin_ref, y_out,
                 xn_s, qt_s, e1_s, e2_s, th_s, acc_s, *slabs, te):
    tm = x1_ref.shape[0]
    ng = tm // LANE
    nb = te // LANE
    j = pl.program_id(1)

    @pl.when(j == 0)
    def _route():
        xn = (_rms(x1_ref[...]) * nffn_ref[...]).astype(BF16)
        xn_s[...] = xn
        qt_s[...] = _dot_nt(wqt_ref[...], xn).astype(BF16)
        acc_s[...] = jnp.zeros_like(acc_s)
        for h in range(PEER_HEADS):
            for p, e_s in ((0, e1_s), (1, e2_s)):
                r0 = (2 * h + p) * PEER_HALF
                s = _dot(keys_ref[p], qt_s[r0:r0 + PEER_HALF, :])
                e = jnp.exp(s - _col_max(s))
                for g in range(ng):
                    e_s[g, h] = e[:, g * LANE:(g + 1) * LANE]

        def route_one(g, h):
            e1 = e1_s[g, h]
            e2 = e2_s[g, h]
            v1 = _top_values(e1, PEER_TOPK)
            v2 = _top_values(e2, PEER_TOPK)
            half = PEER_TOPK // 2

            def products(w2):
                return ([v1[:half] * w2[b:b + 1] for b in range(PEER_TOPK)], v1[half:] * w2[0:1])

            cand = products(v2)
            th = _merge_heads(*cand, PEER_TOPK)[-1]
            sel = ([c >= th for c in cand[0]], cand[1] >= th)

            def selected(prod, fill, combine):
                out = jnp.where(sel[1], prod[1], fill)
                for s, c in zip(sel[0], prod[0]):
                    out = combine(out, jnp.where(s, c, fill))
                return out

            inv_z = 1.0 / jnp.sum(selected(cand, 0.0, jnp.add), axis=0, keepdims=True)
            e2_s[g, h] = e2 * inv_z
            thn = jnp.min(selected(products(v2 * inv_z), jnp.inf, jnp.minimum), axis=0, keepdims=True)
            th_s[g, h] = jnp.broadcast_to(thn, (SUBLANE, LANE))

        def route_group(i, _):
            for k in range(ROUTE_UNROLL):
                item = ROUTE_UNROLL * i + k
                route_one(item // PEER_HEADS, item % PEER_HEADS)
            return 0

        lax.fori_loop(0, ng * PEER_HEADS // ROUTE_UNROLL, route_group, 0)

    def hidden(c):
        slabs[2 * c][...] = _dot_nt(u_ref[c * 2 * LANE:(c + 1) * 2 * LANE, :], xn_s[...])

    def accumulate(c):
        acc_s[...] += _dot(vt_ref[0, :, c * 2 * LANE:(c + 1) * 2 * LANE], slabs[2 * c + 1][...])

    hidden(0)
    for c in range(nb // 2):
        ht_s, a_s = slabs[2 * c], slabs[2 * c + 1]
        if c + 1 < nb // 2:
            hidden(c + 1)
        for i in range(2):
            r = slice(i * LANE, (i + 1) * LANE)
            i1 = j * nb + 2 * c + i
            for g in range(ng):
                cols = slice(g * LANE, (g + 1) * LANE)
                w = jnp.zeros((LANE, LANE), F32)
                for h in range(PEER_HEADS):
                    z = e1_s[g, h, pl.ds(i1, 1), :] * e2_s[g, h]
                    w = w + jnp.where(z >= th_s[g, h, 0:1, :], z, 0.0)
                a_s[r, cols] = (w * _gelu(ht_s[r, cols])).astype(BF16)
        if c >= 1:
            accumulate(c - 1)
    accumulate(nb // 2 - 1)

    @pl.when(j == pl.num_programs(1) - 1)
    def _finish():
        x2 = x1_ref[...] + acc_s[...].T
        y_out[...] = _rms(x2) * nfin_ref[...]


def _peer(x1, w, tm):
    T, D = x1.shape
    NE = w["u"].shape[0]
    te = w["vt"].shape[2]
    assert T % tm == 0 and NE % te == 0 and tm % LANE == 0 and te % (2 * LANE) == 0
    ng = tm // LANE
    consts = [w["nffn"], w["wqt"], w["keys"]]
    return pl.pallas_call(
        functools.partial(_peer_kernel, te=te),
        grid=(T // tm, NE // te),
        in_specs=[pl.BlockSpec((tm, D), lambda i, j: (i, 0))]
        + [_const_spec(c.shape) for c in consts]
        + [pl.BlockSpec((te, D), lambda i, j: (j, 0)),
           pl.BlockSpec((1, D, te), lambda i, j: (j, 0, 0)),
           _const_spec(w["nfin"].shape)],
        out_specs=pl.BlockSpec((tm, D), lambda i, j: (i, 0)),
        out_shape=jax.ShapeDtypeStruct((T, D), F32),
        scratch_shapes=[
            pltpu.VMEM((tm, D), BF16),
            pltpu.VMEM((PEER_HEADS * 2 * PEER_HALF, tm), BF16),
            pltpu.VMEM((ng, PEER_HEADS, N_KEYS, LANE), F32),
            pltpu.VMEM((ng, PEER_HEADS, N_KEYS, LANE), F32),
            pltpu.VMEM((ng, PEER_HEADS, SUBLANE, LANE), F32),
            pltpu.VMEM((D, tm), F32),
        ] + [pltpu.VMEM((2 * LANE, tm), F32),
             pltpu.VMEM((2 * LANE, tm), BF16)] * (te // (2 * LANE)),
        compiler_params=pltpu.CompilerParams(
            dimension_semantics=("parallel", "arbitrary"), vmem_limit_bytes=VMEM_LIMIT),
        name="peer",
    )(x1, *consts, w["u"], w["vt"], w["nfin"])


def _rope_tables(seq):
    inv = 1.0 / (ROPE_THETA ** (jnp.arange(0, QK_ROPE, 2, dtype=F32) / QK_ROPE))
    ang = jnp.arange(seq, dtype=F32)[:, None] * inv[None, :]
    cos, sin = jnp.cos(ang), jnp.sin(ang)
    pad = jnp.zeros((seq, LANE - QK_ROPE), F32)
    return (jnp.concatenate([cos, cos, pad], axis=1), jnp.concatenate([sin, sin, pad], axis=1))


def _rot_cols(wpe):
    half = QK_ROPE // 2
    return jnp.concatenate([-wpe[..., half:], wpe[..., :half]], axis=-1)


def _prep_weights(norm_mix, w_in, q_norm, kv_norm, w_uq, w_ukv, w_o_b, sgu_norm, sgu_w, sgu_b,
                  w_o_a, w_out, norm_ffn, peer_wq, peer_keys, peer_u, peer_v, norm_final):
    a, b, c = Q_LORA, Q_LORA + KV_LORA, Q_LORA + KV_LORA + QK_ROPE
    d, e, f = c + SGU_WIDTH, c + 2 * SGU_WIDTH, c + 2 * SGU_WIDTH + D_MODEL
    zpad = lambda rows, n: jnp.zeros((rows, n), F32)
    w_kr = w_in[:, b:c]
    wkr = jnp.concatenate([w_kr, zpad(D_MODEL, LANE - QK_ROPE),
                           _rot_cols(w_kr), zpad(D_MODEL, LANE - QK_ROPE)], axis=1)
    uq = w_uq.reshape(Q_LORA, MLA_HEADS, QK_NOPE + QK_ROPE)
    hpad = jnp.zeros((Q_LORA, MLA_HEADS, LANE - QK_ROPE), F32)
    wuq = jnp.concatenate([uq, hpad], axis=-1).reshape(Q_LORA, MLA_HEADS * QK_PAD)
    wuqr = jnp.concatenate([_rot_cols(uq[..., QK_NOPE:]), hpad], axis=-1).reshape(Q_LORA, MLA_HEADS * LANE)
    ukv = w_ukv.reshape(KV_LORA, MLA_HEADS, QK_NOPE + V_HEAD)
    scale = (QK_NOPE + QK_ROPE) ** -0.5 * LOG2E
    sbias = jnp.repeat(sgu_b.T, SGU_WIDTH // SGU_GROUPS, axis=1)
    row = lambda v: v.reshape(1, -1).astype(F32)
    return {
        "nmix": row(norm_mix), "wq": w_in[:, :a].astype(BF16), "wkv": w_in[:, a:b].astype(BF16),
        "wkr": wkr.astype(BF16), "wu": w_in[:, c:d].astype(BF16), "wv": w_in[:, d:e].astype(BF16),
        "wga": w_in[:, e:f].astype(BF16), "wgb": w_in[:, f:].astype(BF16),
        "qn": row(q_norm) * scale, "kvn": row(kv_norm),
        "wuq": wuq.astype(BF16), "wuqr": wuqr.astype(BF16),
        "wuk": ukv[..., :QK_NOPE].reshape(KV_LORA, MLA_HEADS * QK_NOPE).astype(BF16),
        "wuvt": ukv[..., QK_NOPE:].reshape(KV_LORA, MLA_HEADS * V_HEAD).T.astype(BF16),
        "sgn": row(sgu_norm), "sw": sgu_w.astype(BF16), "sbias": sbias.astype(F32),
        "woa": w_o_a.astype(BF16), "wob": w_o_b.astype(BF16), "wout": w_out.astype(BF16),
        "nffn": row(norm_ffn), "wqt": peer_wq.T.astype(BF16), "keys": peer_keys.astype(BF16),
        "u": peer_u.astype(BF16), "nfin": row(norm_final),
        "vt": peer_v.reshape(-1, PEER_TE, D_MODEL).transpose(0, 2, 1).astype(BF16),
    }


def _trunk(x, w, front_tm=FRONT_TM, tq=ATTN_TQ, tk=ATTN_TK, post_tm=POST_TM,
           peer_tm=PEER_TM):
    B, S, D = x.shape
    cos, sin = _rope_tables(S)
    q, k, vt, ya, sgb = _front(x, cos, sin, w, front_tm)
    o = _attention(q, k, vt, tq, tk)
    flat = lambda t: t.reshape(B * S, t.shape[-1])
    x1 = _post(flat(x), flat(o), flat(ya), flat(sgb), w["wob"], w["wout"], post_tm)
    y = _peer(x1, w, peer_tm)
    return y.reshape(B, S, D)


def kernel(x_prompt, x_sample, norm_mix, w_in, q_norm, kv_norm, w_uq, w_ukv, w_o_b, sgu_norm,
           sgu_w, sgu_b, w_o_a, w_out, norm_ffn, peer_wq, peer_keys, peer_u, peer_v, norm_final):
    assert norm_mix.shape[0] == 1, "single layer"
    w = _prep_weights(norm_mix[0], w_in[0], q_norm[0], kv_norm[0], w_uq[0], w_ukv[0], w_o_b[0],
                      sgu_norm[0], sgu_w[0], sgu_b[0], w_o_a[0], w_out[0], norm_ffn[0],
                      peer_wq[0], peer_keys[0], peer_u[0], peer_v[0], norm_final)
    return (_trunk(x_prompt, w), _trunk(x_sample, w))
```

```python
import functools

import jax
import jax.numpy as jnp
from jax import lax
from jax.experimental import pallas as pl
from jax.experimental.pallas import tpu as pltpu

F32 = jnp.float32
BF16 = jnp.bfloat16

D_MODEL = 1024
SGU_CHUNK = 128
SGU_GROUPS = 8
SGU_WIDTH = 1024
MLA_HEADS = 8
QK_NOPE = 128
QK_ROPE = 64
V_HEAD = 128
Q_LORA = 384
KV_LORA = 256
ROPE_THETA = 10000.0
PEER_HEADS = 8
N_KEYS = 128
PEER_HALF = 128
PEER_TOPK = 16
EPS = 1e-6

LANE = 128
SUBLANE = 8
QK_PAD = 256
LOG2E = 1.4426950408889634
VMEM_LIMIT = 58 * 1024 * 1024

FRONT_TM = 512
ATTN_TQ = 1024
ATTN_TK = 1024
ATTN_UNROLL = 4
POST_TM = 1024
PEER_TM = 512
PEER_TE = 4096
SLAB_BUFS = 3
ROUTE_UNROLL = 8


def _rms(x, eps=EPS):
    return x * lax.rsqrt(jnp.mean(x * x, axis=-1, keepdims=True) + eps)


_GELU_K0 = -2.0 * 0.7978845608028654 * LOG2E
_GELU_K1 = _GELU_K0 * 0.044715


def _gelu(x):
    return x / (1.0 + jnp.exp2(x * (_GELU_K0 + _GELU_K1 * (x * x))))


def _dot(a, b):
    return jnp.dot(a, b, preferred_element_type=F32)


def _dot_nt(a, b):
    return lax.dot_general(a, b, (((1,), (1,)), ((), ())), preferred_element_type=F32)


def _const_spec(shape):
    nd = len(shape)
    return pl.BlockSpec(shape, lambda *_: (0,) * nd, pipeline_mode=pl.Buffered(1))


def _front_kernel(x_ref, cos_ref, sin_ref, nmix_ref, wq_ref, wkv_ref, wkr_ref, wu_ref, wv_ref,
                  wga_ref, wgb_ref, qn_ref, kvn_ref, wuq_ref, wuqr_ref, wuk_ref, wuvt_ref,
                  sgn_ref, sw_ref, sbias_ref, woa_ref,
                  q_out, k_out, vt_out, ya_out, sgb_out, vn_s, sgu_s):
    tm = x_ref.shape[1]
    x = x_ref[0]
    xn = (_rms(x) * nmix_ref[...]).astype(BF16)
    cos = cos_ref[...]
    sin = sin_ref[...]

    c_q = (_rms(_dot(xn, wq_ref[...])) * qn_ref[...]).astype(BF16)
    qa = _dot(c_q, wuq_ref[...])
    qr = _dot(c_q, wuqr_ref[...])
    for h in range(MLA_HEADS):
        q_out[0, :, h * QK_PAD:h * QK_PAD + LANE] = qa[:, h * QK_PAD:h * QK_PAD + LANE].astype(BF16)
        pe = qa[:, h * QK_PAD + LANE:(h + 1) * QK_PAD] * cos + qr[:, h * LANE:(h + 1) * LANE] * sin
        q_out[0, :, h * QK_PAD + LANE:(h + 1) * QK_PAD] = pe.astype(BF16)

    c_kv = (_rms(_dot(xn, wkv_ref[...])) * kvn_ref[...]).astype(BF16)
    kn = _dot(c_kv, wuk_ref[...])
    kr = _dot(xn, wkr_ref[...])
    kpe = (kr[:, :LANE] * cos + kr[:, LANE:] * sin).astype(BF16)
    for h in range(MLA_HEADS):
        k_out[0, :, h * QK_PAD:h * QK_PAD + LANE] = kn[:, h * LANE:(h + 1) * LANE].astype(BF16)
        k_out[0, :, h * QK_PAD + LANE:(h + 1) * QK_PAD] = kpe
    vt_out[0] = _dot_nt(wuvt_ref[...], c_kv).astype(BF16)

    v = _gelu(_dot(xn, wv_ref[...]))
    vn_s[...] = (_rms(v) * sgn_ref[...]).astype(BF16)
    u = _gelu(_dot(xn, wu_ref[...]))
    for c in range(tm // SGU_CHUNK):
        rows = slice(c * SGU_CHUNK, (c + 1) * SGU_CHUNK)
        for g in range(SGU_GROUPS):
            cols = slice(g * LANE, (g + 1) * LANE)
            mixed = _dot(sw_ref[g], vn_s[rows, cols]) + sbias_ref[:, cols]
            sgu_s[rows, cols] = (u[rows, cols] * mixed).astype(BF16)
    ya = _dot(sgu_s[...], woa_ref[...])
    ya_out[0] = (jax.nn.sigmoid(_dot(xn, wga_ref[...])) * ya).astype(BF16)
    sgb_out[0] = jax.nn.sigmoid(_dot(xn, wgb_ref[...])).astype(BF16)


def _front(x, cos, sin, w, tm):
    B, S, D = x.shape
    assert S % tm == 0 and tm % SGU_CHUNK == 0
    nt = S // tm
    tok = lambda width: pl.BlockSpec((1, tm, width), lambda b, i: (b, i, 0))
    pos = pl.BlockSpec((tm, LANE), lambda b, i: (i, 0))
    consts = [w["nmix"], w["wq"], w["wkv"], w["wkr"], w["wu"], w["wv"], w["wga"], w["wgb"],
              w["qn"], w["kvn"], w["wuq"], w["wuqr"], w["wuk"], w["wuvt"], w["sgn"], w["sw"],
              w["sbias"], w["woa"]]
    out_shape = (
        jax.ShapeDtypeStruct((B, S, MLA_HEADS * QK_PAD), BF16),
        jax.ShapeDtypeStruct((B, S, MLA_HEADS * QK_PAD), BF16),
        jax.ShapeDtypeStruct((B, MLA_HEADS * V_HEAD, S), BF16),
        jax.ShapeDtypeStruct((B, S, D), BF16),
        jax.ShapeDtypeStruct((B, S, D), BF16),
    )
    return pl.pallas_call(
        _front_kernel,
        grid=(B, nt),
        in_specs=[tok(D), pos, pos] + [_const_spec(c.shape) for c in consts],
        out_specs=(tok(MLA_HEADS * QK_PAD), tok(MLA_HEADS * QK_PAD),
                   pl.BlockSpec((1, MLA_HEADS * V_HEAD, tm), lambda b, i: (b, 0, i)),
                   tok(D), tok(D)),
        out_shape=out_shape,
        scratch_shapes=[pltpu.VMEM((tm, SGU_WIDTH), BF16), pltpu.VMEM((tm, SGU_WIDTH), BF16)],
        compiler_params=pltpu.CompilerParams(
            dimension_semantics=("parallel", "parallel"), vmem_limit_bytes=VMEM_LIMIT),
        name="front",
    )(x, cos, sin, *consts)


def _attn_kernel(q_ref, k_ref, vt_ref, o_ref, sa_s, sb_s, *, tk):
    tq = q_ref.shape[1]
    nk = k_ref.shape[1] // tk
    q = q_ref[0]

    def scores(s_ref, c):
        start = pl.multiple_of(c * tk, tk)
        s = _dot_nt(k_ref[0, pl.ds(start, tk), :], q)
        s_ref[...] = s
        return jnp.max(s, axis=0, keepdims=True)

    def update(s_ref, mc, c, carry):
        m, l, acc = carry
        start = pl.multiple_of(c * tk, tk)
        m_new = jnp.maximum(m, mc)
        alpha = jnp.exp2(m - m_new)
        p = jnp.exp2(s_ref[...] - m_new)
        l = alpha * l + jnp.sum(p, axis=0, keepdims=True)
        acc = alpha * acc + _dot(vt_ref[0, :, pl.ds(start, tk)], p.astype(BF16))
        return m_new, l, acc

    def run(c0, n, mc, state, more):
        bufs = (sa_s, sb_s)
        for t in range(n):
            mc_next = scores(bufs[(t + 1) % 2], c0 + t + 1) if (t + 1 < n or more) else None
            state = update(bufs[t % 2], mc, c0 + t, state)
            mc = mc_next
        return mc, state

    unroll = ATTN_UNROLL if nk >= 4 * ATTN_UNROLL else 2

    def body(i, carry):
        return run(i * unroll, unroll, *carry, more=True)

    state = (jnp.full((1, tq), -jnp.inf, F32), jnp.zeros((1, tq), F32), jnp.zeros((V_HEAD, tq), F32))
    nloop = nk // unroll - 1
    mc, state = lax.fori_loop(0, nloop, body, (scores(sa_s, 0), state))
    _, (_, l, acc) = run(nloop * unroll, unroll, mc, state, more=False)
    o_ref[0] = (acc / l).T.astype(o_ref.dtype)


def _attention(q, k, vt, tq, tk):
    B, S, _ = q.shape
    assert S % tq == 0 and S % (ATTN_UNROLL * tk) == 0 and ATTN_UNROLL % 4 == 0
    return pl.pallas_call(
        functools.partial(_attn_kernel, tk=tk),
        scratch_shapes=[pltpu.VMEM((tk, tq), F32)] * 2,
        grid=(B, MLA_HEADS, S // tq),
        in_specs=[
            pl.BlockSpec((1, tq, QK_PAD), lambda b, h, i: (b, i, h)),
            pl.BlockSpec((1, S, QK_PAD), lambda b, h, i: (b, 0, h)),
            pl.BlockSpec((1, V_HEAD, S), lambda b, h, i: (b, h, 0)),
        ],
        out_specs=pl.BlockSpec((1, tq, V_HEAD), lambda b, h, i: (b, i, h)),
        out_shape=jax.ShapeDtypeStruct((B, S, MLA_HEADS * V_HEAD), BF16),
        compiler_params=pltpu.CompilerParams(
            dimension_semantics=("parallel", "parallel", "arbitrary"),
            vmem_limit_bytes=VMEM_LIMIT),
        name="attn",
    )(q, k, vt)


def _post_kernel(x_ref, o_ref, ya_ref, sgb_ref, wob_ref, wout_ref, x1_out):
    yb = _dot(o_ref[...], wob_ref[...])
    merged = ya_ref[...].astype(F32) + sgb_ref[...].astype(F32) * yb
    x1_out[...] = x_ref[...] + _dot(merged.astype(BF16), wout_ref[...])


def _post(x2, o2, ya2, sgb2, wob, wout, tm):
    T, D = x2.shape
    assert T % tm == 0
    tok = pl.BlockSpec((tm, D), lambda i: (i, 0))
    return pl.pallas_call(
        _post_kernel,
        grid=(T // tm,),
        in_specs=[tok, tok, tok, tok, _const_spec(wob.shape), _const_spec(wout.shape)],
        out_specs=tok,
        out_shape=jax.ShapeDtypeStruct((T, D), F32),
        compiler_params=pltpu.CompilerParams(
            dimension_semantics=("parallel",), vmem_limit_bytes=VMEM_LIMIT),
        name="post",
    )(x2, o2, ya2, sgb2, wob, wout)


def _col_max(a):
    return jnp.max(a, axis=0, keepdims=True)


def _sorting_network(n):
    comps = []
    p = 1
    while p < n:
        k = p
        while k >= 1:
            for j in range(k % p, n - k, 2 * k):
                for i in range(min(k, n - j - k)):
                    if (i + j) // (2 * p) == (i + j + k) // (2 * p):
                        comps.append((i + j, i + j + k))
            k //= 2
        p *= 2
    return comps


def _merge_heads(lists, extra, n):
    lists = list(lists)
    vals = []
    for t in range(n):
        head = lists[0] if extra is None else jnp.maximum(lists[0], extra)
        m = jnp.maximum(_col_max(head), 0.0)
        vals.append(m)
        if t == n - 1:
            break
        hit = lists[0] == m
        live = min(len(lists) - 1, n - 1 - t)
        for d in range(live):
            lists[d] = jnp.where(hit, lists[d + 1], lists[d])
        if live == len(lists) - 1:
            lists[live] = jnp.where(hit, -1.0, lists[live])
        if extra is not None:
            extra = jnp.where(extra == m, -1.0, extra)
    return vals


def _top_values(e, n):
    groups = [e[SUBLANE * r:SUBLANE * (r + 1)] for r in range(e.shape[0] // SUBLANE)]
    for a, b in _sorting_network(len(groups)):
        groups[a], groups[b] = jnp.maximum(groups[a], groups[b]), jnp.minimum(groups[a], groups[b])
    return jnp.concatenate(_merge_heads(groups[:n], None, n), axis=0)


def _peer_kernel(x1_ref, nffn_ref, wqt_ref, keys_ref, u_ref, vt_ref, nfin_ref, y_out,
                 xn_s, qt_s, e1_s, e2_s, th_s, acc_s, *slabs, te):
    tm = x1_ref.shape[0]
    ng = tm // LANE
    nb = te // LANE
    j = pl.program_id(1)

    @pl.when(j == 0)
    def _route():
        xn = (_rms(x1_ref[...]) * nffn_ref[...]).astype(BF16)
        xn_s[...] = xn
        qt_s[...] = _dot_nt(wqt_ref[...], xn).astype(BF16)
        acc_s[...] = jnp.zeros_like(acc_s)
        for h in range(PEER_HEADS):
            for p, e_s in ((0, e1_s), (1, e2_s)):
                r0 = (2 * h + p) * PEER_HALF
                s = _dot(keys_ref[p], qt_s[r0:r0 + PEER_HALF, :])
                e = jnp.exp(s - _col_max(s))
                for g in range(ng):
                    e_s[g, h] = e[:, g * LANE:(g + 1) * LANE]

        def route_one(g, h):
            e1 = e1_s[g, h]
            e2 = e2_s[g, h]
            v1 = _top_values(e1, PEER_TOPK)
            v2 = _top_values(e2, PEER_TOPK)
            half = PEER_TOPK // 2

            def products(w2):
                return ([v1[:half] * w2[b:b + 1] for b in range(PEER_TOPK)], v1[half:] * w2[0:1])

            cand = products(v2)
            th = _merge_heads(*cand, PEER_TOPK)[-1]
            sel = ([c >= th for c in cand[0]], cand[1] >= th)

            def selected(prod, fill, combine):
                out = jnp.where(sel[1], prod[1], fill)
                for s, c in zip(sel[0], prod[0]):
                    out = combine(out, jnp.where(s, c, fill))
                return out

            inv_z = 1.0 / jnp.sum(selected(cand, 0.0, jnp.add), axis=0, keepdims=True)
            e2_s[g, h] = e2 * inv_z
            thn = jnp.min(selected(products(v2 * inv_z), jnp.inf, jnp.minimum), axis=0, keepdims=True)
            th_s[g, h] = jnp.broadcast_to(thn, (SUBLANE, LANE))

        def route_group(i, _):
            for k in range(ROUTE_UNROLL):
                item = ROUTE_UNROLL * i + k
                route_one(item // PEER_HEADS, item % PEER_HEADS)
            return 0

        lax.fori_loop(0, ng * PEER_HEADS // ROUTE_UNROLL, route_group, 0)

    def hidden(c):
        k = c % SLAB_BUFS
        slabs[2 * k][...] = _dot_nt(u_ref[c * 2 * LANE:(c + 1) * 2 * LANE, :], xn_s[...])

    def accumulate(c):
        k = c % SLAB_BUFS
        acc_s[...] += _dot(vt_ref[0, :, c * 2 * LANE:(c + 1) * 2 * LANE], slabs[2 * k + 1][...])

    hidden(0)
    for c in range(nb // 2):
        ht_s, a_s = slabs[2 * (c % SLAB_BUFS)], slabs[2 * (c % SLAB_BUFS) + 1]
        if c + 1 < nb // 2:
            hidden(c + 1)
        for i in range(2):
            r = slice(i * LANE, (i + 1) * LANE)
            i1 = j * nb + 2 * c + i
            for g in range(ng):
                cols = slice(g * LANE, (g + 1) * LANE)
                w = jnp.zeros((LANE, LANE), F32)
                for h in range(PEER_HEADS):
                    z = e1_s[g, h, pl.ds(i1, 1), :] * e2_s[g, h]
                    w = w + jnp.where(z >= th_s[g, h, 0:1, :], z, 0.0)
                a_s[r, cols] = (w * _gelu(ht_s[r, cols])).astype(BF16)
        if c >= 1:
            accumulate(c - 1)
    accumulate(nb // 2 - 1)

    @pl.when(j == pl.num_programs(1) - 1)
    def _finish():
        x2 = x1_ref[...] + acc_s[...].T
        y_out[...] = _rms(x2) * nfin_ref[...]


def _peer(x1, w, tm):
    T, D = x1.shape
    NE = w["u"].shape[0]
    te = w["vt"].shape[2]
    assert T % tm == 0 and NE % te == 0 and tm % LANE == 0 and te % (2 * LANE) == 0
    ng = tm // LANE
    consts = [w["nffn"], w["wqt"], w["keys"]]
    return pl.pallas_call(
        functools.partial(_peer_kernel, te=te),
        grid=(T // tm, NE // te),
        in_specs=[pl.BlockSpec((tm, D), lambda i, j: (i, 0), pipeline_mode=pl.Buffered(1))]
        + [_const_spec(c.shape) for c in consts]
        + [pl.BlockSpec((te, D), lambda i, j: (j, 0)),
           pl.BlockSpec((1, D, te), lambda i, j: (j, 0, 0)),
           _const_spec(w["nfin"].shape)],
        out_specs=pl.BlockSpec((tm, D), lambda i, j: (i, 0)),
        out_shape=jax.ShapeDtypeStruct((T, D), F32),
        scratch_shapes=[
            pltpu.VMEM((tm, D), BF16),
            pltpu.VMEM((PEER_HEADS * 2 * PEER_HALF, tm), BF16),
            pltpu.VMEM((ng, PEER_HEADS, N_KEYS, LANE), F32),
            pltpu.VMEM((ng, PEER_HEADS, N_KEYS, LANE), F32),
            pltpu.VMEM((ng, PEER_HEADS, SUBLANE, LANE), F32),
            pltpu.VMEM((D, tm), F32),
        ] + [pltpu.VMEM((2 * LANE, tm), F32),
             pltpu.VMEM((2 * LANE, tm), BF16)] * SLAB_BUFS,
        compiler_params=pltpu.CompilerParams(
            dimension_semantics=("parallel", "arbitrary"), vmem_limit_bytes=VMEM_LIMIT),
        name="peer",
    )(x1, *consts, w["u"], w["vt"], w["nfin"])


def _rope_tables(seq):
    inv = 1.0 / (ROPE_THETA ** (jnp.arange(0, QK_ROPE, 2, dtype=F32) / QK_ROPE))
    ang = jnp.arange(seq, dtype=F32)[:, None] * inv[None, :]
    cos, sin = jnp.cos(ang), jnp.sin(ang)
    pad = jnp.zeros((seq, LANE - QK_ROPE), F32)
    return (jnp.concatenate([cos, cos, pad], axis=1), jnp.concatenate([sin, sin, pad], axis=1))


def _rot_cols(wpe):
    half = QK_ROPE // 2
    return jnp.concatenate([-wpe[..., half:], wpe[..., :half]], axis=-1)


def _prep_weights(norm_mix, w_in, q_norm, kv_norm, w_uq, w_ukv, w_o_b, sgu_norm, sgu_w, sgu_b,
                  w_o_a, w_out, norm_ffn, peer_wq, peer_keys, peer_u, peer_v, norm_final):
    a, b, c = Q_LORA, Q_LORA + KV_LORA, Q_LORA + KV_LORA + QK_ROPE
    d, e, f = c + SGU_WIDTH, c + 2 * SGU_WIDTH, c + 2 * SGU_WIDTH + D_MODEL
    zpad = lambda rows, n: jnp.zeros((rows, n), F32)
    w_kr = w_in[:, b:c]
    wkr = jnp.concatenate([w_kr, zpad(D_MODEL, LANE - QK_ROPE),
                           _rot_cols(w_kr), zpad(D_MODEL, LANE - QK_ROPE)], axis=1)
    uq = w_uq.reshape(Q_LORA, MLA_HEADS, QK_NOPE + QK_ROPE)
    hpad = jnp.zeros((Q_LORA, MLA_HEADS, LANE - QK_ROPE), F32)
    wuq = jnp.concatenate([uq, hpad], axis=-1).reshape(Q_LORA, MLA_HEADS * QK_PAD)
    wuqr = jnp.concatenate([_rot_cols(uq[..., QK_NOPE:]), hpad], axis=-1).reshape(Q_LORA, MLA_HEADS * LANE)
    ukv = w_ukv.reshape(KV_LORA, MLA_HEADS, QK_NOPE + V_HEAD)
    scale = (QK_NOPE + QK_ROPE) ** -0.5 * LOG2E
    sbias = jnp.repeat(sgu_b.T, SGU_WIDTH // SGU_GROUPS, axis=1)
    row = lambda v: v.reshape(1, -1).astype(F32)
    return {
        "nmix": row(norm_mix), "wq": w_in[:, :a].astype(BF16), "wkv": w_in[:, a:b].astype(BF16),
        "wkr": wkr.astype(BF16), "wu": w_in[:, c:d].astype(BF16), "wv": w_in[:, d:e].astype(BF16),
        "wga": w_in[:, e:f].astype(BF16), "wgb": w_in[:, f:].astype(BF16),
        "qn": row(q_norm) * scale, "kvn": row(kv_norm),
        "wuq": wuq.astype(BF16), "wuqr": wuqr.astype(BF16),
        "wuk": ukv[..., :QK_NOPE].reshape(KV_LORA, MLA_HEADS * QK_NOPE).astype(BF16),
        "wuvt": ukv[..., QK_NOPE:].reshape(KV_LORA, MLA_HEADS * V_HEAD).T.astype(BF16),
        "sgn": row(sgu_norm), "sw": sgu_w.astype(BF16), "sbias": sbias.astype(F32),
        "woa": w_o_a.astype(BF16), "wob": w_o_b.astype(BF16), "wout": w_out.astype(BF16),
        "nffn": row(norm_ffn), "wqt": peer_wq.T.astype(BF16), "keys": peer_keys.astype(BF16),
        "u": peer_u.astype(BF16), "nfin": row(norm_final),
        "vt": peer_v.reshape(-1, PEER_TE, D_MODEL).transpose(0, 2, 1).astype(BF16),
    }


def _trunk(x, w, front_tm=FRONT_TM, tq=ATTN_TQ, tk=ATTN_TK, post_tm=POST_TM,
           peer_tm=PEER_TM):
    B, S, D = x.shape
    cos, sin = _rope_tables(S)
    q, k, vt, ya, sgb = _front(x, cos, sin, w, front_tm)
    o = _attention(q, k, vt, tq, tk)
    flat = lambda t: t.reshape(B * S, t.shape[-1])
    x1 = _post(flat(x), flat(o), flat(ya), flat(sgb), w["wob"], w["wout"], post_tm)
    y = _peer(x1, w, peer_tm)
    return y.reshape(B, S, D)


def kernel(x_prompt, x_sample, norm_mix, w_in, q_norm, kv_norm, w_uq, w_ukv, w_o_b, sgu_norm,
           sgu_w, sgu_b, w_o_a, w_out, norm_ffn, peer_wq, peer_keys, peer_u, peer_v, norm_final):
    assert norm_mix.shape[0] == 1, "single layer"
    w = _prep_weights(norm_mix[0], w_in[0], q_norm[0], kv_norm[0], w_uq[0], w_ukv[0], w_o_b[0],
                      sgu_norm[0], sgu_w[0], sgu_b[0], w_o_a[0], w_out[0], norm_ffn[0],
                      peer_wq[0], peer_keys[0], peer_u[0], peer_v[0], norm_final)
    return (_trunk(x_prompt, w), _trunk(x_sample, w))
```
